```python
import math
import jax, jax.numpy as jnp
from jax import lax
import numpy as np

D_MODEL = 2048
BATCH = 2
SEQ = 8192
DEPTH = 2

HEAD_DIM = 128
D_MIX = D_MODEL
FOX_HEADS = (3 * D_MIX) // (8 * HEAD_DIM)
GDN_HEADS = (3 * D_MIX) // (8 * HEAD_DIM)
FOX_WIDTH = FOX_HEADS * HEAD_DIM
GDN_WIDTH = GDN_HEADS * HEAD_DIM
CONV_CH = D_MIX - FOX_WIDTH - GDN_WIDTH
GDN_CONV = 4
CONF_KERNEL = 31
FOX_BLOCK = 128
GDN_CHUNK = 64
N_META = 16
FRONT = FOX_BLOCK
N_PAD = FRONT - N_META
D_FF = 5632
N_EXPERTS = 8
TOP_K = 2
D_FF_EXPERT = 7168
EPS = 1e-6
MASK_VALUE = -1e30

IN_SPLITS = ([FOX_WIDTH] * 3 + [FOX_HEADS]
             + [GDN_WIDTH] * 3 + [GDN_HEADS, GDN_HEADS, GDN_WIDTH]
             + [2 * CONV_CH])
N_IN = sum(IN_SPLITS)
SPLIT_POINTS = [int(v) for v in np.cumsum(IN_SPLITS)[:-1]]

kernel_name = "hybrid_fox_gdn_conformer_moe"


def rmsnorm(x, g):
    xf = x.astype(jnp.float32)
    y = xf * lax.rsqrt(jnp.mean(xf * xf, axis=-1, keepdims=True) + EPS)
    return (y * g.astype(jnp.float32)).astype(x.dtype)


def l2norm(x):
    return x * lax.rsqrt(jnp.sum(x * x, axis=-1, keepdims=True) + EPS)


def causal_dwconv(x, w):
    K, C = w.shape
    return lax.conv_general_dilated(
        x, w[:, None, :].astype(x.dtype), window_strides=(1,), padding=[(K - 1, 0)],
        dimension_numbers=("NWC", "WIO", "NWC"), feature_group_count=C)


def fox_attention(q, k, v, f_logit, b_f):
    B, L, H, Dh = q.shape
    nb = L // FOX_BLOCK
    qf, kf, vf = (t.astype(jnp.float32) for t in (q, k, v))
    log_f = jax.nn.log_sigmoid(f_logit.astype(jnp.float32) + b_f.astype(jnp.float32))
    c_k = jnp.cumsum(log_f, axis=1).transpose(0, 2, 1)
    pos = jnp.arange(L)
    key_ok = pos >= N_PAD
    q_blocks = qf.reshape(B, nb, FOX_BLOCK, H, Dh).transpose(1, 0, 2, 3, 4)
    c_blocks = c_k.reshape(B, H, nb, FOX_BLOCK).transpose(2, 0, 1, 3)
    starts = jnp.arange(nb) * FOX_BLOCK
    scale = Dh ** -0.5

    def one_block(args):
        q_b, c_b, start = args
        s = (jnp.einsum("bqhd,bkhd->bhqk", q_b, kf) * scale
             + c_b[..., None] - c_k[:, :, None, :])
        q_pos = start + jnp.arange(FOX_BLOCK)
        allowed = (pos[None, :] <= q_pos[:, None]) & key_ok[None, :]
        p = jax.nn.softmax(jnp.where(allowed, s, MASK_VALUE), axis=-1)
        return jnp.einsum("bhqk,bkhd->bqhd", p, vf)

    o = lax.map(one_block, (q_blocks, c_blocks, starts))
    return o.transpose(1, 0, 2, 3, 4).reshape(B, L, H * Dh)


def chunk_gated_delta_rule(q, k, v, g, beta):
    B, L, H, Dk = q.shape
    Dv = v.shape[-1]
    C = GDN_CHUNK
    nc = L // C

    def chunks(t):
        return t.reshape(B, nc, C, H, -1).transpose(0, 3, 1, 2, 4)

    qc, kc, vc = chunks(q), chunks(k), chunks(v)
    gc = g.reshape(B, nc, C, H).transpose(0, 3, 1, 2)
    bc = beta.reshape(B, nc, C, H).transpose(0, 3, 1, 2)
    G = jnp.cumsum(gc, axis=-1)
    incl = jnp.tril(jnp.ones((C, C), dtype=bool))
    strict = jnp.tril(jnp.ones((C, C), dtype=bool), -1)
    diff = G[..., :, None] - G[..., None, :]
    decay = jnp.where(incl, jnp.exp(jnp.where(incl, diff, 0.0)), 0.0)
    kb = kc * bc[..., None]
    vb = vc * bc[..., None]
    lower = jnp.where(strict, jnp.einsum("bhnid,bhnjd->bhnij", kb, kc) * decay, 0.0)
    a_mat = lower + jnp.eye(C, dtype=jnp.float32)
    rhs = jnp.concatenate([vb, kb * jnp.exp(G)[..., None]], axis=-1)
    sol = lax.linalg.triangular_solve(a_mat, rhs, left_side=True, lower=True,
                                      unit_diagonal=True)
    u, w = sol[..., :Dv], sol[..., Dv:]
    qk = jnp.einsum("bhnid,bhnjd->bhnij", qc, kc) * decay
    q_dec = qc * jnp.exp(G)[..., None]
    g_last = G[..., -1]
    k_dec = kc * jnp.exp(g_last[..., None] - G)[..., None]
    xs = tuple(jnp.moveaxis(t, 2, 0) for t in (u, w, qk, q_dec, k_dec, g_last))

    def step(S, inp):
        u_n, w_n, qk_n, qd_n, kd_n, gl_n = inp
        v_new = u_n - jnp.einsum("bhcd,bhde->bhce", w_n, S)
        o_n = (jnp.einsum("bhcd,bhde->bhce", qd_n, S)
               + jnp.einsum("bhij,bhje->bhie", qk_n, v_new))
        S = S * jnp.exp(gl_n)[..., None, None] + jnp.einsum("bhcd,bhce->bhde", kd_n, v_new)
        return S, o_n

    S0 = jnp.zeros((B, H, Dk, Dv), jnp.float32)
    _, o = lax.scan(step, S0, xs)
    return o.transpose(1, 0, 3, 2, 4).reshape(B, L, H, Dv)


def gated_deltanet(hq, hk, hv, ha, hb, hgate, conv_w, A_log, dt_bias, norm_g):
    B, L, _ = hq.shape
    qkv = jax.nn.silu(causal_dwconv(jnp.concatenate([hq, hk, hv], axis=-1), conv_w))
    q, k, v = jnp.split(qkv.astype(jnp.float32), 3, axis=-1)
    q = l2norm(q.reshape(B, L, GDN_HEADS, HEAD_DIM)) * (HEAD_DIM ** -0.5)
    k = l2norm(k.reshape(B, L, GDN_HEADS, HEAD_DIM))
    v = v.reshape(B, L, GDN_HEADS, HEAD_DIM)
    beta = jax.nn.sigmoid(hb.astype(jnp.float32))
    g = -jnp.exp(A_log.astype(jnp.float32)) * jax.nn.softplus(
        ha.astype(jnp.float32) + dt_bias.astype(jnp.float32))
    o = chunk_gated_delta_rule(q, k, v, g, beta)
    gate = hgate.reshape(B, L, GDN_HEADS, HEAD_DIM).astype(jnp.float32)
    o = rmsnorm(o, norm_g) * jax.nn.silu(gate)
    return o.reshape(B, L, GDN_WIDTH).astype(hq.dtype)


def conformer_conv(hc, dw_w, dw_b, ln_g, ln_b):
    a, b = jnp.split(hc, 2, axis=-1)
    u = a * jax.nn.sigmoid(b)
    u = (causal_dwconv(u, dw_w) + dw_b.astype(u.dtype)).astype(jnp.float32)
    mu = jnp.mean(u, axis=-1, keepdims=True)
    var = jnp.mean(jnp.square(u - mu), axis=-1, keepdims=True)
    un = (u - mu) * lax.rsqrt(var + EPS) * ln_g.astype(jnp.float32) + ln_b.astype(jnp.float32)
    return jax.nn.silu(un).astype(hc.dtype)


def swiglu(h, wg, wu, wd):
    return (jax.nn.silu(h @ wg) * (h @ wu)) @ wd


def moe_swiglu(h, router, wg, wu, wd):
    logits = (h @ router).astype(jnp.float32)
    top_vals, top_idx = lax.top_k(logits, TOP_K)
    gates = jax.nn.softmax(top_vals, axis=-1)
    combine = jnp.sum(jax.nn.one_hot(top_idx, N_EXPERTS, dtype=jnp.float32)
                      * gates[..., None], axis=-2)
    out = jnp.zeros(h.shape, jnp.float32)
    for e in range(N_EXPERTS):
        out = out + combine[..., e:e + 1] * swiglu(h, wg[e], wu[e], wd[e]).astype(jnp.float32)
    return out.astype(h.dtype)


def setup_inputs(seed: int = 0) -> dict:
    key = jax.random.key(seed)
    ks = jax.random.split(key, 24)
    n_dense = (DEPTH + 1) // 2
    n_moe = DEPTH // 2
    nrm = jax.random.normal
    f32 = jnp.float32
    dt = jnp.exp(jax.random.uniform(ks[7], (DEPTH, GDN_HEADS), f32,
                                    math.log(1e-3), math.log(1e-1)))
    return {
        "x": nrm(ks[0], (BATCH, SEQ, D_MODEL), f32),
        "meta_tokens": nrm(ks[1], (N_META, D_MODEL), f32),
        "attn_norm_g": 1.0 + 0.02 * nrm(ks[2], (DEPTH, D_MODEL), f32),
        "w_in": nrm(ks[3], (DEPTH, D_MODEL, N_IN), f32) * D_MODEL ** -0.5,
        "fox_b_f": 3.0 + 0.1 * nrm(ks[4], (DEPTH, FOX_HEADS), f32),
        "gdn_conv_w": nrm(ks[5], (DEPTH, GDN_CONV, 3 * GDN_WIDTH), f32) * GDN_CONV ** -0.5,
        "gdn_A_log": jnp.log(jax.random.uniform(ks[6], (DEPTH, GDN_HEADS), f32, 1.0, 16.0)),
        "gdn_dt_bias": dt + jnp.log(-jnp.expm1(-dt)),
        "gdn_norm_g": 1.0 + 0.02 * nrm(ks[8], (DEPTH, HEAD_DIM), f32),
        "cc_dw_w": nrm(ks[9], (DEPTH, CONF_KERNEL, CONV_CH), f32) * CONF_KERNEL ** -0.5,
        "cc_dw_b": 0.02 * nrm(ks[10], (DEPTH, CONV_CH), f32),
        "cc_ln_g": 1.0 + 0.02 * nrm(ks[11], (DEPTH, CONV_CH), f32),
        "cc_ln_b": 0.02 * nrm(ks[12], (DEPTH, CONV_CH), f32),
        "w_out": nrm(ks[13], (DEPTH, D_MIX, D_MODEL), f32) * D_MIX ** -0.5,
        "ffn_norm_g": 1.0 + 0.02 * nrm(ks[14], (DEPTH, D_MODEL), f32),
        "dense_w_gate": nrm(ks[15], (n_dense, D_MODEL, D_FF), f32) * D_MODEL ** -0.5,
        "dense_w_up": nrm(ks[16], (n_dense, D_MODEL, D_FF), f32) * D_MODEL ** -0.5,
        "dense_w_down": nrm(ks[17], (n_dense, D_FF, D_MODEL), f32) * D_FF ** -0.5,
        "moe_router": nrm(ks[18], (n_moe, D_MODEL, N_EXPERTS), f32) * D_MODEL ** -0.5,
        "moe_w_gate": nrm(ks[19], (n_moe, N_EXPERTS, D_MODEL, D_FF_EXPERT), f32) * D_MODEL ** -0.5,
        "moe_w_up": nrm(ks[20], (n_moe, N_EXPERTS, D_MODEL, D_FF_EXPERT), f32) * D_MODEL ** -0.5,
        "moe_w_down": nrm(ks[21], (n_moe, N_EXPERTS, D_FF_EXPERT, D_MODEL), f32) * D_FF_EXPERT ** -0.5,
        "final_norm_g": 1.0 + 0.02 * nrm(ks[22], (D_MODEL,), f32),
    }


def reference(x, meta_tokens, attn_norm_g, w_in, fox_b_f, gdn_conv_w, gdn_A_log,
              gdn_dt_bias, gdn_norm_g, cc_dw_w, cc_dw_b, cc_ln_g, cc_ln_b, w_out,
              ffn_norm_g, dense_w_gate, dense_w_up, dense_w_down, moe_router,
              moe_w_gate, moe_w_up, moe_w_down, final_norm_g):
    B, S, D = x.shape
    pad = jnp.zeros((B, N_PAD, D), x.dtype)
    meta = jnp.broadcast_to(meta_tokens.astype(x.dtype)[None], (B, N_META, D))
    h = jnp.concatenate([pad, meta, x], axis=1)
    L = h.shape[1]
    valid = (jnp.arange(L) >= N_PAD)[None, :, None]

    for layer in range(DEPTH):
        hn = jnp.where(valid, rmsnorm(h, attn_norm_g[layer]), 0)
        z = hn @ w_in[layer]
        (fq, fk, fv, ff, gq, gk, gv, ga, gb, ggate, cc) = jnp.split(z, SPLIT_POINTS, axis=-1)
        fox_o = fox_attention(fq.reshape(B, L, FOX_HEADS, HEAD_DIM),
                              fk.reshape(B, L, FOX_HEADS, HEAD_DIM),
                              fv.reshape(B, L, FOX_HEADS, HEAD_DIM),
                              ff, fox_b_f[layer]).astype(h.dtype)
        gdn_o = gated_deltanet(gq, gk, gv, ga, gb, ggate, gdn_conv_w[layer],
                               gdn_A_log[layer], gdn_dt_bias[layer], gdn_norm_g[layer])
        conv_o = conformer_conv(cc, cc_dw_w[layer], cc_dw_b[layer], cc_ln_g[layer],
                                cc_ln_b[layer])
        mix = jnp.concatenate([fox_o, gdn_o, conv_o], axis=-1)
        h = h + mix @ w_out[layer]

        hn = rmsnorm(h, ffn_norm_g[layer])
        if layer % 2 == 0:
            i = layer // 2
            h = h + swiglu(hn, dense_w_gate[i], dense_w_up[i], dense_w_down[i])
        else:
            i = layer // 2
            h = h + moe_swiglu(hn, moe_router[i], moe_w_gate[i], moe_w_up[i], moe_w_down[i])

    return rmsnorm(h, final_norm_g)[:, FRONT:, :]
```

```python
import functools

import jax
import jax.numpy as jnp
from jax import lax
from jax.experimental import pallas as pl
from jax.experimental.pallas import tpu as pltpu

F32 = jnp.float32
BF16 = jnp.bfloat16

HEAD_DIM = 128
N_HEADS = 6
CONV_CH = 512
GDN_CONV = 4
CONF_KERNEL = 31
GDN_CHUNK = 64
N_META = 16
FRONT = 128
N_PAD = FRONT - N_META
N_EXPERTS = 8
EPS = 1e-6
BIG = 1e30
LANES = 128
CONF_HALO = 32
GDN_HALO = 16
VMEM_LIMIT = 56 * 1024 * 1024

U_CCA, U_CCB, U_FQ, U_FK, U_FV, U_GQ, U_GK, U_GV, U_GG = 0, 4, 8, 14, 20, 26, 32, 38, 44
N_MAIN = 50 * LANES
LANE_FF, LANE_GA, LANE_GB = 0, 6, 12


def _cparams(sem):
    return pltpu.CompilerParams(dimension_semantics=sem, vmem_limit_bytes=VMEM_LIMIT)


def _sigmoid(x):
    return 1.0 / (1.0 + jnp.exp(-x))


def _dot(a, b):
    return jnp.dot(a.astype(BF16), b.astype(BF16), preferred_element_type=F32)


def _dot_nt(a, b):
    return lax.dot_general(a.astype(BF16), b.astype(BF16), (((1,), (1,)), ((), ())),
                           preferred_element_type=F32)


def _dot_tn(a, b):
    return lax.dot_general(a.astype(BF16), b.astype(BF16), (((0,), (0,)), ((), ())),
                           preferred_element_type=F32)


def _split3(x):
    hi = x.astype(BF16)
    r1 = x - hi.astype(F32)
    mid = r1.astype(BF16)
    lo = (r1 - mid.astype(F32)).astype(BF16)
    return hi, mid, lo


def _rms_rows(x, g):
    ms = jnp.mean(x * x, axis=-1, keepdims=True)
    return x * lax.rsqrt(ms + EPS) * g


def _inproj_kernel(h_ref, g_ref, w_ref, ws_ref, z_ref, zs_ref, xn_ref, *, tm, tiles_per_seq):
    i = pl.program_id(0)
    j = pl.program_id(1)

    @pl.when(j == 0)
    def _():
        y = _rms_rows(h_ref[...], g_ref[...])
        pos = (i % tiles_per_seq) * tm + lax.broadcasted_iota(jnp.int32, (tm, 1), 0)
        xb = jnp.where(pos >= N_PAD, y, 0.0).astype(BF16)
        xn_ref[...] = xb
        zs_ref[...] = jnp.dot(xb, ws_ref[...], preferred_element_type=F32)

    z_ref[...] = jnp.dot(xn_ref[...], w_ref[...], preferred_element_type=F32).astype(z_ref.dtype)


def _inproj(h, g, w_main, w_small, *, seq_len, tm, tn):
    M, D = h.shape
    N = w_main.shape[1]
    kern = functools.partial(_inproj_kernel, tm=tm, tiles_per_seq=seq_len // tm)
    return pl.pallas_call(
        kern,
        grid=(M // tm, N // tn),
        in_specs=[
            pl.BlockSpec((tm, D), lambda i, j: (i, 0)),
            pl.BlockSpec((1, D), lambda i, j: (0, 0)),
            pl.BlockSpec((D, tn), lambda i, j: (0, j)),
            pl.BlockSpec((D, LANES), lambda i, j: (0, 0)),
        ],
        out_specs=[
            pl.BlockSpec((tm, tn), lambda i, j: (i, j)),
            pl.BlockSpec((tm, LANES), lambda i, j: (i, 0)),
        ],
        out_shape=[jax.ShapeDtypeStruct((M, N), BF16), jax.ShapeDtypeStruct((M, LANES), F32)],
        scratch_shapes=[pltpu.VMEM((tm, D), BF16)],
        compiler_params=_cparams(("parallel", "arbitrary")),
        name="inproj",
    )(h, g, w_main, w_small)


def _gates_kernel(zs_ref, bias_ref, alog_ref, out_ref, carry_ref, *, T):
    t = pl.program_id(1)

    @pl.when(t == 0)
    def _():
        carry_ref[...] = jnp.zeros_like(carry_ref)

    x = zs_ref[0] + bias_ref[...]
    lane = lax.broadcasted_iota(jnp.int32, (T, LANES), 1)
    tail = jnp.log(1.0 + jnp.exp(-jnp.abs(x)))
    log_sig = jnp.minimum(x, 0.0) - tail
    softplus = jnp.maximum(x, 0.0) + tail
    is_f = lane < LANE_GA
    is_g = (lane >= LANE_GA) & (lane < LANE_GB)
    is_b = (lane >= LANE_GB) & (lane < LANE_GB + N_HEADS)
    val_f = jnp.where(is_f, log_sig, 0.0)
    val_g = jnp.where(is_g, -jnp.exp(alog_ref[...]) * softplus, 0.0)

    row = lax.broadcasted_iota(jnp.int32, (T, T), 0)
    col = lax.broadcasted_iota(jnp.int32, (T, T), 1)
    tri_full = (col <= row).astype(BF16)
    tri_chunk = ((col <= row) & ((col // GDN_CHUNK) == (row // GDN_CHUNK))).astype(BF16)

    def cumsum(tri, v):
        hi, mid, lo = _split3(v)
        return (jnp.dot(tri, hi, preferred_element_type=F32)
                + jnp.dot(tri, mid, preferred_element_type=F32)
                + jnp.dot(tri, lo, preferred_element_type=F32))

    c = cumsum(tri_full, val_f) + carry_ref[...]
    carry_ref[...] = c[T - 1:T, :]
    gcum = cumsum(tri_chunk, val_g)
    out_ref[0] = jnp.where(is_f, c, jnp.where(is_g, gcum, jnp.where(is_b, _sigmoid(x), 0.0)))


def _gates(zs, bias_vec, alog_vec, *, T):
    B, L, _ = zs.shape
    return pl.pallas_call(
        functools.partial(_gates_kernel, T=T),
        grid=(B, L // T),
        in_specs=[
            pl.BlockSpec((1, T, LANES), lambda b, t: (b, t, 0)),
            pl.BlockSpec((1, LANES), lambda b, t: (0, 0)),
            pl.BlockSpec((1, LANES), lambda b, t: (0, 0)),
        ],
        out_specs=pl.BlockSpec((1, T, LANES), lambda b, t: (b, t, 0)),
        out_shape=jax.ShapeDtypeStruct((B, L, LANES), F32),
        scratch_shapes=[pltpu.VMEM((1, LANES), F32)],
        compiler_params=_cparams(("parallel", "arbitrary")),
        name="gates",
    )(zs, bias_vec, alog_vec)


def _fox_kernel(q_ref, k_ref, v_ref, crow_ref, o_ref, *, T):
    h = pl.program_id(1)
    qi = pl.program_id(2)
    q = q_ref[0]

    def scores(kb):
        start = pl.multiple_of(kb * T, T)
        k = k_ref[0, pl.ds(start, T), :]
        ck = crow_ref[0, h, pl.ds(kb, 1), :]
        pos = kb * T + lax.broadcasted_iota(jnp.int32, (1, T), 1)
        ck = jnp.where(pos < N_PAD, BIG, ck)
        s = lax.dot_general(q, k, (((1,), (1,)), ((), ())), preferred_element_type=F32)
        return s - ck, start

    def update(s, start, carry):
        m, l, acc = carry
        v = v_ref[0, pl.ds(start, T), :]
        m_new = jnp.maximum(m, jnp.max(s, axis=1, keepdims=True))
        alpha = jnp.exp(m - m_new)
        p = jnp.exp(s - m_new)
        l = alpha * l + jnp.sum(p, axis=1, keepdims=True)
        acc = alpha * acc + jnp.dot(p.astype(BF16), v, preferred_element_type=F32)
        return m_new, l, acc

    def full_block(kb, carry):
        s, start = scores(kb)
        return update(s, start, carry)

    init = (jnp.full((T, 1), -3.0e38, F32), jnp.zeros((T, 1), F32), jnp.zeros((T, HEAD_DIM), F32))
    carry = lax.fori_loop(0, qi, full_block, init)
    s, start = scores(qi)
    row = lax.broadcasted_iota(jnp.int32, (T, T), 0)
    col = lax.broadcasted_iota(jnp.int32, (T, T), 1)
    s = jnp.where(col <= row, s, -BIG)
    m, l, acc = update(s, start, carry)
    o_ref[0] = (acc / l).astype(o_ref.dtype)


def _fox(zb, crow, *, T):
    B, L, _ = zb.shape
    nblk = L // T
    return pl.pallas_call(
        functools.partial(_fox_kernel, T=T),
        grid=(B, N_HEADS, nblk),
        in_specs=[
            pl.BlockSpec((1, T, HEAD_DIM), lambda b, h, i: (b, i, U_FQ + h)),
            pl.BlockSpec((1, L, HEAD_DIM), lambda b, h, i: (b, 0, U_FK + h)),
            pl.BlockSpec((1, L, HEAD_DIM), lambda b, h, i: (b, 0, U_FV + h)),
            pl.BlockSpec((1, crow.shape[1], nblk, T), lambda b, h, i: (b, 0, 0, 0)),
        ],
        out_specs=pl.BlockSpec((1, T, HEAD_DIM), lambda b, h, i: (b, i, h)),
        out_shape=jax.ShapeDtypeStruct((B, L, N_HEADS * HEAD_DIM), BF16),
        compiler_params=_cparams(("parallel", "parallel", "arbitrary")),
        name="fox",
    )(zb, zb, zb, crow)


def _gdn_kernel(q_ref, k_ref, v_ref, qh_ref, kh_ref, vh_ref, wq_ref, wk_ref, wv_ref, gt_ref, gr_ref,
                gate_ref, ng_ref, o_ref, S_ref, buf, q_s, k_s, kb_s, vb_s, kg_s, gb_s, o_s, *, T):
    h = pl.program_id(1)
    t = pl.program_id(2)
    C = GDN_CHUNK
    nchunk = T // C

    @pl.when(t == 0)
    def _():
        S_ref[...] = jnp.zeros_like(S_ref)

    def conv_silu(x_ref, halo_ref, w_ref):
        halo = jnp.where(t > 0, halo_ref[0].astype(F32), 0.0)
        buf[0:GDN_HALO, :] = halo
        buf[GDN_HALO:GDN_HALO + T, :] = x_ref[0].astype(F32)
        w = w_ref[0]
        y = w[0:1, :] * buf[pl.ds(GDN_HALO - GDN_CONV + 1, T), :]
        for i in range(1, GDN_CONV):
            y = y + w[i:i + 1, :] * buf[pl.ds(GDN_HALO - GDN_CONV + 1 + i, T), :]
        return y * _sigmoid(y)

    def l2n(x):
        return x * lax.rsqrt(jnp.sum(x * x, axis=-1, keepdims=True) + EPS)

    q = l2n(conv_silu(q_ref, qh_ref, wq_ref)) * (HEAD_DIM ** -0.5)
    k = l2n(conv_silu(k_ref, kh_ref, wk_ref))
    v = conv_silu(v_ref, vh_ref, wv_ref)

    gt = gt_ref[0]
    lane = lax.broadcasted_iota(jnp.int32, (T, LANES), 1)
    g_col = jnp.sum(jnp.where(lane == LANE_GA + h, gt, 0.0), axis=1, keepdims=True)
    beta = jnp.sum(jnp.where(lane == LANE_GB + h, gt, 0.0), axis=1, keepdims=True)
    g_b = jnp.broadcast_to(g_col, (T, HEAD_DIM))
    kb = k * beta
    q_s[...] = q
    k_s[...] = k
    kb_s[...] = kb
    vb_s[...] = v * beta
    kg_s[...] = kb * jnp.exp(g_b)
    gb_s[...] = g_b

    row = lax.broadcasted_iota(jnp.int32, (C, C), 0)
    col = lax.broadcasted_iota(jnp.int32, (C, C), 1)
    incl = col <= row
    strict = col < row
    eye = (col == row).astype(F32)

    def chunk(c, S):
        r0 = pl.multiple_of(c * C, C)
        kc = k_s[pl.ds(r0, C), :]
        kbc = kb_s[pl.ds(r0, C), :]
        gbc = gb_s[pl.ds(r0, C), :]
        g_row = gr_ref[0, LANE_GA + h, pl.ds(t * nchunk + c, 1), :]
        g_last = gb_s[pl.ds(r0 + C - 1, 1), :]
        diff = gbc[:, 0:C] - g_row
        decay = jnp.where(incl, jnp.exp(jnp.where(incl, diff, 0.0)), 0.0)
        n1 = jnp.where(strict, _dot_nt(kbc, kc) * decay, 0.0)
        p = eye - n1
        npow = n1
        for _ in range(5):
            npow = _dot(npow, npow)
            p = p + _dot(p, npow)
        rhs = jnp.concatenate([vb_s[pl.ds(r0, C), :], kg_s[pl.ds(r0, C), :]], axis=1)
        sol = _dot(p, rhs)
        u = sol[:, :HEAD_DIM]
        w = sol[:, HEAD_DIM:]
        qc = q_s[pl.ds(r0, C), :]
        qk = _dot_nt(qc, kc) * decay
        v_new = u - _dot(w, S)
        o_s[pl.ds(r0, C), :] = _dot(qc * jnp.exp(gbc), S) + _dot(qk, v_new)
        k_dec = kc * jnp.exp(g_last - gbc)
        return S * jnp.exp(g_last) + _dot_tn(k_dec, v_new)

    S_ref[...] = lax.fori_loop(0, nchunk, chunk, S_ref[...])

    o = o_s[...]
    y = o * lax.rsqrt(jnp.mean(o * o, axis=-1, keepdims=True) + EPS) * ng_ref[...]
    gate = gate_ref[0].astype(F32)
    o_ref[0] = (y * (gate * _sigmoid(gate))).astype(o_ref.dtype)


def _gdn(zb, gates, grow, conv_w, norm_g, *, T):
    B, L, _ = zb.shape
    nblk = L // T
    hb = T // GDN_HALO

    def cur(u):
        return pl.BlockSpec((1, T, HEAD_DIM), lambda b, h, t: (b, t, u + h))

    def halo(u):
        return pl.BlockSpec((1, GDN_HALO, HEAD_DIM),
                            lambda b, h, t: (b, jnp.maximum(t * hb - 1, 0), u + h))

    def cw(j):
        return pl.BlockSpec((1, GDN_CONV, HEAD_DIM), lambda b, h, t: (j * N_HEADS + h, 0, 0))

    vm = lambda r: pltpu.VMEM((r, HEAD_DIM), F32)
    return pl.pallas_call(
        functools.partial(_gdn_kernel, T=T),
        grid=(B, N_HEADS, nblk),
        in_specs=[
            cur(U_GQ), cur(U_GK), cur(U_GV), halo(U_GQ), halo(U_GK), halo(U_GV),
            cw(0), cw(1), cw(2),
            pl.BlockSpec((1, T, LANES), lambda b, h, t: (b, t, 0)),
            pl.BlockSpec((1, grow.shape[1], L // GDN_CHUNK, GDN_CHUNK), lambda b, h, t: (b, 0, 0, 0)),
            cur(U_GG),
            pl.BlockSpec((1, HEAD_DIM), lambda b, h, t: (0, 0)),
        ],
        out_specs=pl.BlockSpec((1, T, HEAD_DIM), lambda b, h, t: (b, t, h)),
        out_shape=jax.ShapeDtypeStruct((B, L, N_HEADS * HEAD_DIM), BF16),
        scratch_shapes=[vm(HEAD_DIM), vm(T + GDN_HALO)] + [vm(T)] * 7,
        compiler_params=_cparams(("parallel", "parallel", "arbitrary")),
        name="gdn",
    )(zb, zb, zb, zb, zb, zb, conv_w, conv_w, conv_w, gates, grow, zb, norm_g)


def _conf_kernel(a_ref, b_ref, ah_ref, bh_ref, w_ref, wb_ref, lg_ref, lb_ref, o_ref, ubuf, *, T, RC):
    t = pl.program_id(1)
    ah = ah_ref[0].astype(F32)
    bh = bh_ref[0].astype(F32)
    ubuf[0:CONF_HALO, :] = jnp.where(t > 0, ah * _sigmoid(bh), 0.0)
    a = a_ref[0].astype(F32)
    b = b_ref[0].astype(F32)
    ubuf[CONF_HALO:CONF_HALO + T, :] = a * _sigmoid(b)
    base = CONF_HALO - CONF_KERNEL + 1
    for r0 in range(0, T, RC):
        acc = jnp.broadcast_to(wb_ref[...], (RC, CONV_CH))
        for i in range(CONF_KERNEL):
            acc = acc + w_ref[i:i + 1, :] * ubuf[pl.ds(r0 + base + i, RC), :]
        mu = jnp.mean(acc, axis=-1, keepdims=True)
        d = acc - mu
        var = jnp.mean(d * d, axis=-1, keepdims=True)
        un = d * lax.rsqrt(var + EPS) * lg_ref[...] + lb_ref[...]
        o_ref[0, pl.ds(r0, RC), :] = (un * _sigmoid(un)).astype(o_ref.dtype)


def _conf(zb, dw_w, dw_b, ln_g, ln_b, *, T, RC=64):
    B, L, _ = zb.shape
    hb = T // CONF_HALO
    ua, ub = U_CCA * LANES // CONV_CH, U_CCB * LANES // CONV_CH
    vec = pl.BlockSpec((1, CONV_CH), lambda b, t: (0, 0))

    def halo(u):
        return pl.BlockSpec((1, CONF_HALO, CONV_CH), lambda b, t: (b, jnp.maximum(t * hb - 1, 0), u))

    return pl.pallas_call(
        functools.partial(_conf_kernel, T=T, RC=RC),
        grid=(B, L // T),
        in_specs=[
            pl.BlockSpec((1, T, CONV_CH), lambda b, t: (b, t, ua)),
            pl.BlockSpec((1, T, CONV_CH), lambda b, t: (b, t, ub)),
            halo(ua), halo(ub),
            pl.BlockSpec((CONF_KERNEL, CONV_CH), lambda b, t: (0, 0)),
            vec, vec, vec,
        ],
        out_specs=pl.BlockSpec((1, T, CONV_CH), lambda b, t: (b, t, 0)),
        out_shape=jax.ShapeDtypeStruct((B, L, CONV_CH), BF16),
        scratch_shapes=[pltpu.VMEM((T + CONF_HALO, CONV_CH), F32)],
        compiler_params=_cparams(("parallel", "parallel")),
        name="conf",
    )(zb, zb, zb, zb, dw_w, dw_b, ln_g, ln_b)


def _outproj_kernel(fox_ref, gdn_ref, conf_ref, wf_ref, wg_ref, wc_ref, h_ref, o_ref):
    acc = jnp.dot(fox_ref[...], wf_ref[...], preferred_element_type=F32)
    acc = acc + jnp.dot(gdn_ref[...], wg_ref[...], preferred_element_type=F32)
    acc = acc + jnp.dot(conf_ref[...], wc_ref[...], preferred_element_type=F32)
    o_ref[...] = h_ref[...] + acc


def _outproj(fox_o, gdn_o, conf_o, w_out, h, *, tm, tn):
    M, D = h.shape
    W = N_HEADS * HEAD_DIM
    return pl.pallas_call(
        _outproj_kernel,
        grid=(D // tn, M // tm),
        in_specs=[
            pl.BlockSpec((tm, W), lambda j, i: (i, 0)),
            pl.BlockSpec((tm, W), lambda j, i: (i, 0)),
            pl.BlockSpec((tm, CONV_CH), lambda j, i: (i, 0)),
            pl.BlockSpec((W, tn), lambda j, i: (0, j)),
            pl.BlockSpec((W, tn), lambda j, i: (1, j)),
            pl.BlockSpec((CONV_CH, tn), lambda j, i: (2 * W // CONV_CH, j)),
            pl.BlockSpec((tm, tn), lambda j, i: (i, j)),
        ],
        out_specs=pl.BlockSpec((tm, tn), lambda j, i: (i, j)),
        out_shape=jax.ShapeDtypeStruct((M, D), F32),
        compiler_params=_cparams(("parallel", "parallel")),
        name="outproj",
    )(fox_o, gdn_o, conf_o, w_out, w_out, w_out, h)


def _ffn_kernel(h_ref, g_ref, wg_ref, wu_ref, wd_ref, o_ref, xn_ref):
    f = pl.program_id(1)

    @pl.when(f == 0)
    def _():
        x = h_ref[...]
        xn_ref[...] = _rms_rows(x, g_ref[...]).astype(BF16)
        o_ref[...] = x

    xn = xn_ref[...]
    gate = jnp.dot(xn, wg_ref[...], preferred_element_type=F32)
    up = jnp.dot(xn, wu_ref[...], preferred_element_type=F32)
    act = (gate * _sigmoid(gate) * up).astype(BF16)
    o_ref[...] += jnp.dot(act, wd_ref[...], preferred_element_type=F32)


def _ffn(h, g, wg, wu, wd, *, tm, tf):
    M, D = h.shape
    F = wg.shape[1]
    return pl.pallas_call(
        _ffn_kernel,
        grid=(M // tm, F // tf),
        in_specs=[
            pl.BlockSpec((tm, D), lambda i, f: (i, 0)),
            pl.BlockSpec((1, D), lambda i, f: (0, 0)),
            pl.BlockSpec((D, tf), lambda i, f: (0, f)),
            pl.BlockSpec((D, tf), lambda i, f: (0, f)),
            pl.BlockSpec((tf, D), lambda i, f: (f, 0)),
        ],
        out_specs=pl.BlockSpec((tm, D), lambda i, f: (i, 0)),
        out_shape=jax.ShapeDtypeStruct((M, D), F32),
        scratch_shapes=[pltpu.VMEM((tm, D), BF16)],
        compiler_params=_cparams(("parallel", "arbitrary")),
        name="ffn",
    )(h, g, wg, wu, wd)


def _router_kernel(h_ref, g_ref, rhi_ref, rlo_ref, info_ref, cnt_ref, carry_ref, *, tm):
    i = pl.program_id(0)

    @pl.when(i == 0)
    def _():
        carry_ref[...] = jnp.zeros_like(carry_ref)

    xn = _rms_rows(h_ref[...], g_ref[...])
    x_hi = xn.astype(BF16)
    x_lo = (xn - x_hi.astype(F32)).astype(BF16)
    logits = (jnp.dot(x_hi, rhi_ref[...], preferred_element_type=F32)
              + jnp.dot(x_lo, rhi_ref[...], preferred_element_type=F32)
              + jnp.dot(x_hi, rlo_ref[...], preferred_element_type=F32))
    lane = lax.broadcasted_iota(jnp.int32, (tm, LANES), 1).astype(F32)
    logits = jnp.where(lane < N_EXPERTS, logits, -BIG)
    m1 = jnp.max(logits, axis=1, keepdims=True)
    i1 = jnp.min(jnp.where(logits == m1, lane, float(LANES)), axis=1, keepdims=True)
    rest = jnp.where(lane == i1, -BIG, logits)
    m2 = jnp.max(rest, axis=1, keepdims=True)
    i2 = jnp.min(jnp.where(rest == m2, lane, float(LANES)), axis=1, keepdims=True)
    e = jnp.exp(m2 - m1)
    g1 = 1.0 / (1.0 + e)
    g2 = e / (1.0 + e)

    sel = ((lane == i1) | (lane == i2)).astype(F32)
    row = lax.broadcasted_iota(jnp.int32, (tm, tm), 0)
    col = lax.broadcasted_iota(jnp.int32, (tm, tm), 1)
    before = jnp.dot((col < row).astype(BF16), sel.astype(BF16), preferred_element_type=F32)
    ranks = before + carry_ref[...]
    r1 = jnp.sum(jnp.where(lane == i1, ranks, 0.0), axis=1, keepdims=True)
    r2 = jnp.sum(jnp.where(lane == i2, ranks, 0.0), axis=1, keepdims=True)
    carry_ref[...] += jnp.sum(sel, axis=0, keepdims=True)
    cnt_ref[...] = carry_ref[...]

    info = jnp.where(lane == 0, i1, 0.0)
    info = jnp.where(lane == 1, i2, info)
    info = jnp.where(lane == 2, g1, info)
    info = jnp.where(lane == 3, g2, info)
    info = jnp.where(lane == 4, r1, info)
    info = jnp.where(lane == 5, r2, info)
    info_ref[...] = info


def _router(h, g, r_hi, r_lo, *, tm):
    M, D = h.shape
    return pl.pallas_call(
        functools.partial(_router_kernel, tm=tm),
        grid=(M // tm,),
        in_specs=[
            pl.BlockSpec((tm, D), lambda i: (i, 0)),
            pl.BlockSpec((1, D), lambda i: (0, 0)),
            pl.BlockSpec((D, LANES), lambda i: (0, 0)),
            pl.BlockSpec((D, LANES), lambda i: (0, 0)),
        ],
        out_specs=[
            pl.BlockSpec((tm, LANES), lambda i: (i, 0)),
            pl.BlockSpec((1, LANES), lambda i: (0, 0)),
        ],
        out_shape=[jax.ShapeDtypeStruct((M, LANES), F32), jax.ShapeDtypeStruct((1, LANES), F32)],
        scratch_shapes=[pltpu.VMEM((1, LANES), F32)],
        compiler_params=_cparams(("arbitrary",)),
        name="router",
    )(h, g, r_hi, r_lo)


PACK_SHIFT = 15
PACK = 1 << PACK_SHIFT


def _moe_kernel(te_ref, nv_ref, pk_ref, h_hbm, g_ref, wg_ref, wu_ref, wd_ref, y_hbm,
                xg, xn, acc, sem_g, sem_s, *, tm, nf):
    i = pl.program_id(0)
    f = pl.program_id(1)
    valid = i < nv_ref[0]

    def gather_copy(r):
        tok = pk_ref[i * tm + r] & (PACK - 1)
        return pltpu.make_async_copy(h_hbm.at[pl.ds(tok, 1), :], xg.at[pl.ds(r, 1), :], sem_g)

    def scatter_copy(r):
        dst = pk_ref[i * tm + r] >> PACK_SHIFT
        return pltpu.make_async_copy(acc.at[pl.ds(r, 1), :], y_hbm.at[pl.ds(dst, 1), :], sem_s)

    @pl.when(valid & (f == 0))
    def _():
        def start(r, c):
            gather_copy(r).start()
            return c

        def wait(r, c):
            gather_copy(r).wait()
            return c

        lax.fori_loop(0, tm, start, 0)
        lax.fori_loop(0, tm, wait, 0)
        xn[...] = _rms_rows(xg[...], g_ref[...]).astype(BF16)
        acc[...] = jnp.zeros_like(acc)

    @pl.when(valid)
    def _():
        x = xn[...]
        gate = jnp.dot(x, wg_ref[0], preferred_element_type=F32)
        up = jnp.dot(x, wu_ref[0], preferred_element_type=F32)
        act = (gate * _sigmoid(gate) * up).astype(BF16)
        acc[...] += jnp.dot(act, wd_ref[0], preferred_element_type=F32)

    @pl.when(valid & (f == nf - 1))
    def _():
        def start(r, c):
            scatter_copy(r).start()
            return c

        def wait(r, c):
            scatter_copy(r).wait()
            return c

        lax.fori_loop(0, tm, start, 0)
        lax.fori_loop(0, tm, wait, 0)


def _moe(tile_expert, n_valid, packed, h, g, wg, wu, wd, *, tm, tf, n_tiles):
    M, D = h.shape
    F = wg.shape[2]
    nf = F // tf

    def fblk(i, f, nv):
        return jnp.where(i < nv[0], f, nf - 1)

    return pl.pallas_call(
        functools.partial(_moe_kernel, tm=tm, nf=nf),
        grid_spec=pltpu.PrefetchScalarGridSpec(
            num_scalar_prefetch=3,
            grid=(n_tiles, nf),
            in_specs=[
                pl.BlockSpec(memory_space=pl.ANY),
                pl.BlockSpec((1, D), lambda i, f, te, nv, pk: (0, 0)),
                pl.BlockSpec((1, D, tf), lambda i, f, te, nv, pk: (te[i], 0, fblk(i, f, nv))),
                pl.BlockSpec((1, D, tf), lambda i, f, te, nv, pk: (te[i], 0, fblk(i, f, nv))),
                pl.BlockSpec((1, tf, D), lambda i, f, te, nv, pk: (te[i], fblk(i, f, nv), 0)),
            ],
            out_specs=pl.BlockSpec(memory_space=pl.ANY),
            scratch_shapes=[
                pltpu.VMEM((tm, D), F32), pltpu.VMEM((tm, D), BF16), pltpu.VMEM((tm, D), F32),
                pltpu.SemaphoreType.DMA, pltpu.SemaphoreType.DMA,
            ],
        ),
        out_shape=jax.ShapeDtypeStruct((2 * M + tm, D), F32),
        compiler_params=_cparams(("arbitrary", "arbitrary")),
        name="moe",
    )(tile_expert, n_valid, packed, h, g, wg, wu, wd)


def _combine_kernel(h_ref, y0_ref, y1_ref, info_ref, gf_ref, o_ref):
    info = info_ref[...]
    hn = h_ref[...] + info[:, 2:3] * y0_ref[...] + info[:, 3:4] * y1_ref[...]
    o_ref[...] = _rms_rows(hn, gf_ref[...])


def _combine(h, y, info, gf, *, batch, seq_len):
    M, D = h.shape
    tb = FRONT
    per_seq = seq_len // tb
    real = per_seq - 1

    def src(off):
        return lambda b, j: (off + b * per_seq + 1 + j, 0)

    return pl.pallas_call(
        _combine_kernel,
        grid=(batch, real),
        in_specs=[
            pl.BlockSpec((tb, D), src(0)),
            pl.BlockSpec((tb, D), src(0)),
            pl.BlockSpec((tb, D), src(M // tb)),
            pl.BlockSpec((tb, LANES), src(0)),
            pl.BlockSpec((1, D), lambda b, j: (0, 0)),
        ],
        out_specs=pl.BlockSpec((tb, D), lambda b, j: (b * real + j, 0)),
        out_shape=jax.ShapeDtypeStruct((batch * real * tb, D), F32),
        compiler_params=_cparams(("parallel", "parallel")),
        name="combine",
    )(h, y, y, info, gf)


def _prep_w_in(w):
    W = N_HEADS * HEAD_DIM
    sizes = [W, W, W, N_HEADS, W, W, W, N_HEADS, N_HEADS, W, 2 * CONV_CH]
    parts, o = [], 0
    for s in sizes:
        parts.append(w[:, o:o + s])
        o += s
    fq, fk, fv, ff, gq, gk, gv, ga, gb, gg, cc = parts
    main = jnp.concatenate([cc, fq * (HEAD_DIM ** -0.5), fk, fv, gq, gk, gv, gg], axis=1).astype(BF16)
    small = jnp.concatenate([ff, ga, gb], axis=1)
    small = jnp.pad(small, ((0, 0), (0, LANES - small.shape[1]))).astype(BF16)
    return main, small


def _lane_vec(pairs):
    v = jnp.zeros((1, LANES), F32)
    for lane0, vals in pairs:
        v = lax.dynamic_update_slice(v, vals.astype(F32)[None, :], (0, lane0))
    return v


def _mixer(h, layer, p, *, B, L, T, tm):
    zb, zs = _inproj(h, p["attn_norm_g"][layer][None, :], *p["w_in"][layer], seq_len=L, tm=tm, tn=1280)
    zb = zb.reshape(B, L, N_MAIN)
    bias_vec = _lane_vec([(LANE_FF, p["fox_b_f"][layer]), (LANE_GA, p["gdn_dt_bias"][layer])])
    alog_vec = _lane_vec([(LANE_GA, p["gdn_A_log"][layer])])
    gates = _gates(zs.reshape(B, L, LANES), bias_vec, alog_vec, T=T)
    gates_t = jnp.swapaxes(gates[:, :, :24], 1, 2)
    crow = gates_t.reshape(B, 24, L // T, T)
    grow = gates_t.reshape(B, 24, L // GDN_CHUNK, GDN_CHUNK)
    fox_o = _fox(zb, crow, T=T)
    conv_w = p["gdn_conv_w"][layer].reshape(GDN_CONV, 3 * N_HEADS, HEAD_DIM).transpose(1, 0, 2)
    gdn_o = _gdn(zb, gates, grow, conv_w, p["gdn_norm_g"][layer][None, :], T=T)
    conf_o = _conf(zb, p["cc_dw_w"][layer], p["cc_dw_b"][layer][None, :], p["cc_ln_g"][layer][None, :],
                   p["cc_ln_b"][layer][None, :], T=T)
    M = B * L
    return _outproj(fox_o.reshape(M, -1), gdn_o.reshape(M, -1), conf_o.reshape(M, -1),
                    p["w_out"][layer], h, tm=tm, tn=1024)


def _moe_layer(h, i, p, *, B, L, tm_r, tm_e, tf_e):
    M, D = h.shape
    g = p["ffn_norm_g_moe"][i][None, :]
    router = jnp.pad(p["moe_router"][i], ((0, 0), (0, LANES - N_EXPERTS)))
    r_hi = router.astype(BF16)
    r_lo = (router - r_hi.astype(F32)).astype(BF16)
    info, cnt = _router(h, g, r_hi, r_lo, tm=tm_r)

    e = info[:, 0:2].astype(jnp.int32)
    rank = info[:, 4:6].astype(jnp.int32)
    counts = cnt[0, :N_EXPERTS].astype(jnp.int32)
    padded = ((counts + tm_e - 1) // tm_e) * tm_e
    ends = jnp.cumsum(padded)
    offs = ends - padded
    pos = offs[e] + rank
    n_tiles = (2 * M + N_EXPERTS * (tm_e - 1)) // tm_e
    P = n_tiles * tm_e
    tok = jnp.arange(M, dtype=jnp.int32)[:, None]
    dst = jnp.arange(2, dtype=jnp.int32)[None, :] * M + tok
    word = dst * PACK + tok
    filler = (2 * M + jnp.arange(P, dtype=jnp.int32) % tm_e) * PACK
    packed = filler.at[pos.reshape(-1)].set(word.reshape(-1))
    n_valid = (ends[-1] // tm_e).astype(jnp.int32)
    tile_start = jnp.minimum(jnp.arange(n_tiles, dtype=jnp.int32), n_valid - 1) * tm_e
    tile_expert = jnp.minimum(jnp.searchsorted(ends, tile_start, side="right"),
                              N_EXPERTS - 1).astype(jnp.int32)

    y = _moe(tile_expert, n_valid[None], packed, h, g, p["moe_w_gate"][i], p["moe_w_up"][i],
             p["moe_w_down"][i], tm=tm_e, tf=tf_e, n_tiles=n_tiles)
    return _combine(h, y, info, p["final_norm_g"][None, :], batch=B, seq_len=L)


def kernel(x, meta_tokens, attn_norm_g, w_in, fox_b_f, gdn_conv_w, gdn_A_log, gdn_dt_bias, gdn_norm_g,
           cc_dw_w, cc_dw_b, cc_ln_g, cc_ln_b, w_out, ffn_norm_g, dense_w_gate, dense_w_up,
           dense_w_down, moe_router, moe_w_gate, moe_w_up, moe_w_down, final_norm_g):
    B, S, D = x.shape
    depth = w_in.shape[0]
    assert depth == 2, "layer 0 is dense SwiGLU, layer 1 is the routed expert SwiGLU followed by the final norm"
    L = S + FRONT
    T = 640 if L % 640 == 0 else 128
    tm = T
    p = dict(
        attn_norm_g=attn_norm_g, fox_b_f=fox_b_f, gdn_conv_w=gdn_conv_w, gdn_A_log=gdn_A_log,
        gdn_dt_bias=gdn_dt_bias, gdn_norm_g=gdn_norm_g, cc_dw_w=cc_dw_w, cc_dw_b=cc_dw_b,
        cc_ln_g=cc_ln_g, cc_ln_b=cc_ln_b, w_out=w_out.astype(BF16),
        w_in=[_prep_w_in(w_in[l]) for l in range(depth)],
        ffn_norm_g_moe=ffn_norm_g[1::2], moe_router=moe_router,
        moe_w_gate=moe_w_gate.astype(BF16), moe_w_up=moe_w_up.astype(BF16),
        moe_w_down=moe_w_down.astype(BF16), final_norm_g=final_norm_g,
    )
    pad = jnp.zeros((B, N_PAD, D), x.dtype)
    meta = jnp.broadcast_to(meta_tokens.astype(x.dtype)[None], (B, N_META, D))
    h = jnp.concatenate([pad, meta, x], axis=1).reshape(B * L, D)

    h = _mixer(h, 0, p, B=B, L=L, T=T, tm=tm)
    h = _ffn(h, ffn_norm_g[0][None, :], dense_w_gate[0].astype(BF16), dense_w_up[0].astype(BF16),
             dense_w_down[0].astype(BF16), tm=tm, tf=512)
    h = _mixer(h, 1, p, B=B, L=L, T=T, tm=tm)
    out = _moe_layer(h, 0, p, B=B, L=L, tm_r=tm, tm_e=512, tf_e=512)
    return out.reshape(B, S, D)
```

```python
import functools

import jax
import jax.numpy as jnp
from jax import lax
from jax.experimental import pallas as pl
from jax.experimental.pallas import tpu as pltpu

F32 = jnp.float32
BF16 = jnp.bfloat16

HEAD_DIM = 128
N_HEADS = 6
CONV_CH = 512
GDN_CONV = 4
CONF_KERNEL = 31
GDN_CHUNK = 64
N_META = 16
FRONT = 128
N_PAD = FRONT - N_META
N_EXPERTS = 8
EPS = 1e-6
BIG = 1e30
LANES = 128
CONF_HALO = 32
GDN_HALO = 16
VMEM_LIMIT = 56 * 1024 * 1024

U_CCA, U_CCB, U_FQ, U_FK, U_FV, U_GQ, U_GK, U_GV, U_GG = 0, 4, 8, 14, 20, 26, 32, 38, 44
N_MAIN = 50 * LANES
LANE_FF, LANE_GA, LANE_GB = 0, 6, 12


def _cparams(sem):
    return pltpu.CompilerParams(dimension_semantics=sem, vmem_limit_bytes=VMEM_LIMIT)


def _sigmoid(x):
    return 1.0 / (1.0 + jnp.exp(-x))


def _dot(a, b):
    return jnp.dot(a.astype(BF16), b.astype(BF16), preferred_element_type=F32)


def _dot_nt(a, b):
    return lax.dot_general(a.astype(BF16), b.astype(BF16), (((1,), (1,)), ((), ())),
                           preferred_element_type=F32)


def _dot_tn(a, b):
    return lax.dot_general(a.astype(BF16), b.astype(BF16), (((0,), (0,)), ((), ())),
                           preferred_element_type=F32)


def _split3(x):
    hi = x.astype(BF16)
    r1 = x - hi.astype(F32)
    mid = r1.astype(BF16)
    lo = (r1 - mid.astype(F32)).astype(BF16)
    return hi, mid, lo


def _rms_rows(x, g):
    ms = jnp.mean(x * x, axis=-1, keepdims=True)
    return x * lax.rsqrt(ms + EPS) * g


def _inproj_kernel(h_ref, g_ref, w_ref, ws_ref, z_ref, zs_ref, xn_ref, *, tm, tiles_per_seq):
    i = pl.program_id(0)
    j = pl.program_id(1)

    @pl.when(j == 0)
    def _():
        y = _rms_rows(h_ref[...], g_ref[...])
        pos = (i % tiles_per_seq) * tm + lax.broadcasted_iota(jnp.int32, (tm, 1), 0)
        xb = jnp.where(pos >= N_PAD, y, 0.0).astype(BF16)
        xn_ref[...] = xb
        zs_ref[...] = jnp.dot(xb, ws_ref[...], preferred_element_type=F32)

    z_ref[...] = jnp.dot(xn_ref[...], w_ref[...], preferred_element_type=F32).astype(z_ref.dtype)


def _inproj(h, g, w_main, w_small, *, seq_len, tm, tn):
    M, D = h.shape
    N = w_main.shape[1]
    kern = functools.partial(_inproj_kernel, tm=tm, tiles_per_seq=seq_len // tm)
    return pl.pallas_call(
        kern,
        grid=(M // tm, N // tn),
        in_specs=[
            pl.BlockSpec((tm, D), lambda i, j: (i, 0)),
            pl.BlockSpec((1, D), lambda i, j: (0, 0)),
            pl.BlockSpec((D, tn), lambda i, j: (0, j)),
            pl.BlockSpec((D, LANES), lambda i, j: (0, 0)),
        ],
        out_specs=[
            pl.BlockSpec((tm, tn), lambda i, j: (i, j)),
            pl.BlockSpec((tm, LANES), lambda i, j: (i, 0)),
        ],
        out_shape=[jax.ShapeDtypeStruct((M, N), BF16), jax.ShapeDtypeStruct((M, LANES), F32)],
        scratch_shapes=[pltpu.VMEM((tm, D), BF16)],
        compiler_params=_cparams(("parallel", "arbitrary")),
        name="inproj",
    )(h, g, w_main, w_small)


def _gates_kernel(zs_ref, bias_ref, alog_ref, out_ref, carry_ref, *, T):
    t = pl.program_id(1)

    @pl.when(t == 0)
    def _():
        carry_ref[...] = jnp.zeros_like(carry_ref)

    x = zs_ref[0] + bias_ref[...]
    lane = lax.broadcasted_iota(jnp.int32, (T, LANES), 1)
    tail = jnp.log(1.0 + jnp.exp(-jnp.abs(x)))
    log_sig = jnp.minimum(x, 0.0) - tail
    softplus = jnp.maximum(x, 0.0) + tail
    is_f = lane < LANE_GA
    is_g = (lane >= LANE_GA) & (lane < LANE_GB)
    is_b = (lane >= LANE_GB) & (lane < LANE_GB + N_HEADS)
    val_f = jnp.where(is_f, log_sig, 0.0)
    val_g = jnp.where(is_g, -jnp.exp(alog_ref[...]) * softplus, 0.0)

    row = lax.broadcasted_iota(jnp.int32, (T, T), 0)
    col = lax.broadcasted_iota(jnp.int32, (T, T), 1)
    tri_full = (col <= row).astype(BF16)
    tri_chunk = ((col <= row) & ((col // GDN_CHUNK) == (row // GDN_CHUNK))).astype(BF16)

    def cumsum(tri, v):
        hi, mid, lo = _split3(v)
        return (jnp.dot(tri, hi, preferred_element_type=F32)
                + jnp.dot(tri, mid, preferred_element_type=F32)
                + jnp.dot(tri, lo, preferred_element_type=F32))

    c = cumsum(tri_full, val_f) + carry_ref[...]
    carry_ref[...] = c[T - 1:T, :]
    gcum = cumsum(tri_chunk, val_g)
    out_ref[0] = jnp.where(is_f, c, jnp.where(is_g, gcum, jnp.where(is_b, _sigmoid(x), 0.0)))


def _gates(zs, bias_vec, alog_vec, *, T):
    B, L, _ = zs.shape
    return pl.pallas_call(
        functools.partial(_gates_kernel, T=T),
        grid=(B, L // T),
        in_specs=[
            pl.BlockSpec((1, T, LANES), lambda b, t: (b, t, 0)),
            pl.BlockSpec((1, LANES), lambda b, t: (0, 0)),
            pl.BlockSpec((1, LANES), lambda b, t: (0, 0)),
        ],
        out_specs=pl.BlockSpec((1, T, LANES), lambda b, t: (b, t, 0)),
        out_shape=jax.ShapeDtypeStruct((B, L, LANES), F32),
        scratch_shapes=[pltpu.VMEM((1, LANES), F32)],
        compiler_params=_cparams(("parallel", "arbitrary")),
        name="gates",
    )(zs, bias_vec, alog_vec)


LOG2E = 1.4426950408889634
AUG_STRIDE = 8


def _fox_kernel(q_ref, k_ref, v_ref, g_ref, o_ref, kaug, *, T, nsub):
    h = pl.program_id(1)
    qi = pl.program_id(2)
    L = k_ref.shape[1]
    Ts = T // nsub

    @pl.when(qi == 0)
    def _():
        def build(j, c):
            r0 = pl.multiple_of(j * T, T)
            pos = r0 + lax.broadcasted_iota(jnp.int32, (T, LANES), 0)
            lane = lax.broadcasted_iota(jnp.int32, (T, LANES), 1)
            ck = jnp.where(pos < N_PAD, BIG, g_ref[0, pl.ds(r0, T), :] * LOG2E)
            hi, mid, lo = (x.astype(F32) for x in _split3(ck))
            ext = jnp.where(lane < AUG_STRIDE, hi,
                            jnp.where(lane < 2 * AUG_STRIDE, pltpu.roll(mid, AUG_STRIDE, 1),
                                      jnp.where(lane < 3 * AUG_STRIDE, pltpu.roll(lo, 2 * AUG_STRIDE, 1), 0.0)))
            kaug[pl.ds(r0, T), 0:HEAD_DIM] = k_ref[0, pl.ds(r0, T), :]
            kaug[pl.ds(r0, T), HEAD_DIM:2 * HEAD_DIM] = ext.astype(BF16)
            return c
        lax.fori_loop(0, L // T, build, 0)

    lane = lax.broadcasted_iota(jnp.int32, (Ts, LANES), 1)
    pick = (lane == h) | (lane == h + AUG_STRIDE) | (lane == h + 2 * AUG_STRIDE)
    minus_one = jnp.where(pick, -1.0, 0.0).astype(BF16)
    qs = [jnp.concatenate([q_ref[0, j * Ts:(j + 1) * Ts, :], minus_one], axis=1) for j in range(nsub)]

    def key_block(kb):
        start = pl.multiple_of(kb * T, T)
        return kaug[pl.ds(start, T), :], v_ref[0, pl.ds(start, T), :]

    def update(q, k, v, carry, mask=None):
        m, l, acc = carry
        s = lax.dot_general(q, k, (((1,), (1,)), ((), ())), preferred_element_type=F32)
        if mask is not None:
            s = jnp.where(mask, s, -BIG)
        m_new = jnp.maximum(m, jnp.max(s, axis=1, keepdims=True))
        alpha = jnp.exp2(m - m_new)
        p = jnp.exp2(s - m_new)
        l = alpha * l + jnp.sum(p, axis=1, keepdims=True)
        acc = alpha * acc + jnp.dot(p.astype(BF16), v, preferred_element_type=F32)
        return m_new, l, acc

    def full_block(kb, carries):
        k, v = key_block(kb)
        return tuple(update(qs[j], k, v, carries[j]) for j in range(nsub))

    init = tuple((jnp.full((Ts, 1), -3.0e38, F32), jnp.zeros((Ts, 1), F32),
                  jnp.zeros((Ts, HEAD_DIM), F32)) for _ in range(nsub))
    carries = lax.fori_loop(0, qi, full_block, init)
    k, v = key_block(qi)
    row = lax.broadcasted_iota(jnp.int32, (Ts, T), 0)
    col = lax.broadcasted_iota(jnp.int32, (Ts, T), 1)
    for j in range(nsub):
        m, l, acc = update(qs[j], k, v, carries[j], mask=col <= row + j * Ts)
        o_ref[0, j * Ts:(j + 1) * Ts, :] = (acc / l).astype(o_ref.dtype)


def _fox(zb, gates, *, T, nsub=1):
    B, L, _ = zb.shape
    nblk = L // T
    assert LANE_FF + N_HEADS <= AUG_STRIDE
    return pl.pallas_call(
        functools.partial(_fox_kernel, T=T, nsub=nsub),
        grid=(B, N_HEADS, nblk),
        in_specs=[
            pl.BlockSpec((1, T, HEAD_DIM), lambda b, h, i: (b, i, U_FQ + h)),
            pl.BlockSpec((1, L, HEAD_DIM), lambda b, h, i: (b, 0, U_FK + h)),
            pl.BlockSpec((1, L, HEAD_DIM), lambda b, h, i: (b, 0, U_FV + h)),
            pl.BlockSpec((1, L, LANES), lambda b, h, i: (b, 0, 0)),
        ],
        out_specs=pl.BlockSpec((1, T, HEAD_DIM), lambda b, h, i: (b, i, h)),
        out_shape=jax.ShapeDtypeStruct((B, L, N_HEADS * HEAD_DIM), BF16),
        scratch_shapes=[pltpu.VMEM((L, 2 * HEAD_DIM), BF16)],
        compiler_params=_cparams(("parallel", "parallel", "arbitrary")),
        name="fox",
    )(zb, zb, zb, gates)


def _gdn_kernel(q_ref, k_ref, v_ref, qh_ref, kh_ref, vh_ref, wq_ref, wk_ref, wv_ref, gt_ref, gr_ref,
                gate_ref, ng_ref, o_ref, S_ref, buf, *, T, HG):
    hp = pl.program_id(1)
    t = pl.program_id(2)
    C = GDN_CHUNK
    nchunk = T // C

    @pl.when(t == 0)
    def _():
        S_ref[...] = jnp.zeros_like(S_ref)

    def conv_silu(x_ref, halo_ref, w_ref, j):
        lanes = slice(j * HEAD_DIM, (j + 1) * HEAD_DIM)
        halo = jnp.where(t > 0, halo_ref[0, :, lanes].astype(F32), 0.0)
        buf[0:GDN_HALO, :] = halo
        buf[GDN_HALO:GDN_HALO + T, :] = x_ref[0, :, lanes].astype(F32)
        w = w_ref[j]
        y = w[0:1, :] * buf[pl.ds(GDN_HALO - GDN_CONV + 1, T), :]
        for i in range(1, GDN_CONV):
            y = y + w[i:i + 1, :] * buf[pl.ds(GDN_HALO - GDN_CONV + 1 + i, T), :]
        return y * _sigmoid(y)

    def l2n(x):
        return x * lax.rsqrt(jnp.sum(x * x, axis=-1, keepdims=True) + EPS)

    gt = gt_ref[0]
    lane = lax.broadcasted_iota(jnp.int32, (T, LANES), 1)
    q, k, g_b, kb, vb, kg, qd = [], [], [], [], [], [], []
    for j in range(HG):
        h = hp * HG + j
        qj = l2n(conv_silu(q_ref, qh_ref, wq_ref, j)) * (HEAD_DIM ** -0.5)
        kj = l2n(conv_silu(k_ref, kh_ref, wk_ref, j))
        vj = conv_silu(v_ref, vh_ref, wv_ref, j)
        g_col = jnp.sum(jnp.where(lane == LANE_GA + h, gt, 0.0), axis=1, keepdims=True)
        beta = jnp.sum(jnp.where(lane == LANE_GB + h, gt, 0.0), axis=1, keepdims=True)
        gj = jnp.broadcast_to(g_col, (T, HEAD_DIM))
        eg = jnp.exp(gj)
        q.append(qj)
        k.append(kj)
        g_b.append(gj)
        kb.append(kj * beta)
        vb.append(vj * beta)
        kg.append(kj * beta * eg)
        qd.append(qj * eg)

    row = lax.broadcasted_iota(jnp.int32, (C, C), 0)
    col = lax.broadcasted_iota(jnp.int32, (C, C), 1)
    incl = col <= row
    strict = col < row
    eye = (col == row).astype(F32)

    ps = [(j, c) for c in range(nchunk) for j in range(HG)]
    rows = lambda x, jc: x[jc[0]][jc[1] * C:(jc[1] + 1) * C]
    g_last = {jc: g_b[jc[0]][(jc[1] + 1) * C - 1:(jc[1] + 1) * C] for jc in ps}
    decay = {}
    for jc in ps:
        g_row = gr_ref[0, LANE_GA + hp * HG + jc[0], pl.ds(t * nchunk + jc[1], 1), :]
        diff = rows(g_b, jc)[:, 0:C] - g_row
        decay[jc] = jnp.where(incl, jnp.exp(jnp.where(incl, diff, 0.0)), 0.0)
    kq = {jc: _dot_nt(jnp.concatenate([rows(kb, jc), rows(q, jc)], axis=0), rows(k, jc)) for jc in ps}
    npow = {jc: jnp.where(strict, kq[jc][:C] * decay[jc], 0.0) for jc in ps}
    qk = {jc: kq[jc][C:] * decay[jc] for jc in ps}
    p = {jc: eye - npow[jc] for jc in ps}
    for _ in range(5):
        npow = {jc: _dot(npow[jc], npow[jc]) for jc in ps}
        p = {jc: p[jc] + _dot(p[jc], npow[jc]) for jc in ps}
    wu = {jc: _dot(p[jc], jnp.concatenate([rows(kg, jc), rows(vb, jc)], axis=1)) for jc in ps}
    kwu = {jc: _dot_tn(rows(k, jc) * jnp.exp(g_last[jc] - rows(g_b, jc)), wu[jc]) for jc in ps}
    qwu = {jc: _dot(qk[jc], wu[jc]) for jc in ps}
    S = [S_ref[j] for j in range(HG)]
    outs = [[] for _ in range(HG)]
    for jc in ps:
        j = jc[0]
        lhs = jnp.concatenate([kwu[jc][:, :HEAD_DIM], rows(qd, jc) - qwu[jc][:, :HEAD_DIM]], axis=0)
        xs = _dot(lhs, S[j])
        outs[j].append(xs[HEAD_DIM:] + qwu[jc][:, HEAD_DIM:])
        S[j] = S[j] * jnp.exp(g_last[jc]) - xs[:HEAD_DIM] + kwu[jc][:, HEAD_DIM:]

    for j in range(HG):
        S_ref[j] = S[j]
        lanes = slice(j * HEAD_DIM, (j + 1) * HEAD_DIM)
        o = jnp.concatenate(outs[j], axis=0)
        y = o * lax.rsqrt(jnp.mean(o * o, axis=-1, keepdims=True) + EPS) * ng_ref[...]
        gate = gate_ref[0, :, lanes].astype(F32)
        o_ref[0, :, lanes] = (y * (gate * _sigmoid(gate))).astype(o_ref.dtype)


def _gdn(zb, gates, grow, conv_w, norm_g, *, T, HG=2):
    B, L, _ = zb.shape
    nblk = L // T
    hb = T // GDN_HALO
    W = HG * HEAD_DIM

    def cur(u):
        return pl.BlockSpec((1, T, W), lambda b, h, t: (b, t, u // HG + h))

    def halo(u):
        return pl.BlockSpec((1, GDN_HALO, W),
                            lambda b, h, t: (b, jnp.maximum(t * hb - 1, 0), u // HG + h))

    def cw(j):
        return pl.BlockSpec((HG, GDN_CONV, HEAD_DIM), lambda b, h, t: (j * (N_HEADS // HG) + h, 0, 0))

    assert all(u % HG == 0 for u in (U_GQ, U_GK, U_GV, U_GG)) and N_HEADS % HG == 0
    return pl.pallas_call(
        functools.partial(_gdn_kernel, T=T, HG=HG),
        grid=(B, N_HEADS // HG, nblk),
        in_specs=[
            cur(U_GQ), cur(U_GK), cur(U_GV), halo(U_GQ), halo(U_GK), halo(U_GV),
            cw(0), cw(1), cw(2),
            pl.BlockSpec((1, T, LANES), lambda b, h, t: (b, t, 0)),
            pl.BlockSpec((1, grow.shape[1], L // GDN_CHUNK, GDN_CHUNK), lambda b, h, t: (b, 0, 0, 0)),
            cur(U_GG),
            pl.BlockSpec((1, HEAD_DIM), lambda b, h, t: (0, 0)),
        ],
        out_specs=pl.BlockSpec((1, T, W), lambda b, h, t: (b, t, h)),
        out_shape=jax.ShapeDtypeStruct((B, L, N_HEADS * HEAD_DIM), BF16),
        scratch_shapes=[pltpu.VMEM((HG, HEAD_DIM, HEAD_DIM), F32), pltpu.VMEM((T + GDN_HALO, HEAD_DIM), F32)],
        compiler_params=_cparams(("parallel", "parallel", "arbitrary")),
        name="gdn",
    )(zb, zb, zb, zb, zb, zb, conv_w, conv_w, conv_w, gates, grow, zb, norm_g)


def _conf_kernel(a_ref, b_ref, ah_ref, bh_ref, w_ref, wb_ref, lg_ref, lb_ref, o_ref, ubuf, *, T, RC):
    t = pl.program_id(1)
    ah = ah_ref[0].astype(F32)
    bh = bh_ref[0].astype(F32)
    ubuf[0:CONF_HALO, :] = jnp.where(t > 0, ah * _sigmoid(bh), 0.0)
    a = a_ref[0].astype(F32)
    b = b_ref[0].astype(F32)
    ubuf[CONF_HALO:CONF_HALO + T, :] = a * _sigmoid(b)
    base = CONF_HALO - CONF_KERNEL + 1
    for r0 in range(0, T, RC):
        acc = jnp.broadcast_to(wb_ref[...], (RC, CONV_CH))
        for i in range(CONF_KERNEL):
            acc = acc + w_ref[i:i + 1, :] * ubuf[pl.ds(r0 + base + i, RC), :]
        mu = jnp.mean(acc, axis=-1, keepdims=True)
        d = acc - mu
        var = jnp.mean(d * d, axis=-1, keepdims=True)
        un = d * lax.rsqrt(var + EPS) * lg_ref[...] + lb_ref[...]
        o_ref[0, pl.ds(r0, RC), :] = (un * _sigmoid(un)).astype(o_ref.dtype)


def _conf(zb, dw_w, dw_b, ln_g, ln_b, *, T, RC=64):
    B, L, _ = zb.shape
    hb = T // CONF_HALO
    ua, ub = U_CCA * LANES // CONV_CH, U_CCB * LANES // CONV_CH
    vec = pl.BlockSpec((1, CONV_CH), lambda b, t: (0, 0))

    def halo(u):
        return pl.BlockSpec((1, CONF_HALO, CONV_CH), lambda b, t: (b, jnp.maximum(t * hb - 1, 0), u))

    return pl.pallas_call(
        functools.partial(_conf_kernel, T=T, RC=RC),
        grid=(B, L // T),
        in_specs=[
            pl.BlockSpec((1, T, CONV_CH), lambda b, t: (b, t, ua)),
            pl.BlockSpec((1, T, CONV_CH), lambda b, t: (b, t, ub)),
            halo(ua), halo(ub),
            pl.BlockSpec((CONF_KERNEL, CONV_CH), lambda b, t: (0, 0)),
            vec, vec, vec,
        ],
        out_specs=pl.BlockSpec((1, T, CONV_CH), lambda b, t: (b, t, 0)),
        out_shape=jax.ShapeDtypeStruct((B, L, CONV_CH), BF16),
        scratch_shapes=[pltpu.VMEM((T + CONF_HALO, CONV_CH), F32)],
        compiler_params=_cparams(("parallel", "parallel")),
        name="conf",
    )(zb, zb, zb, zb, dw_w, dw_b, ln_g, ln_b)


def _outproj_kernel(fox_ref, gdn_ref, conf_ref, wf_ref, wg_ref, wc_ref, h_ref, o_ref):
    acc = jnp.dot(fox_ref[...], wf_ref[...], preferred_element_type=F32)
    acc = acc + jnp.dot(gdn_ref[...], wg_ref[...], preferred_element_type=F32)
    acc = acc + jnp.dot(conf_ref[...], wc_ref[...], preferred_element_type=F32)
    o_ref[...] = h_ref[...] + acc


def _outproj(fox_o, gdn_o, conf_o, w_out, h, *, tm, tn):
    M, D = h.shape
    W = N_HEADS * HEAD_DIM
    return pl.pallas_call(
        _outproj_kernel,
        grid=(D // tn, M // tm),
        in_specs=[
            pl.BlockSpec((tm, W), lambda j, i: (i, 0)),
            pl.BlockSpec((tm, W), lambda j, i: (i, 0)),
            pl.BlockSpec((tm, CONV_CH), lambda j, i: (i, 0)),
            pl.BlockSpec((W, tn), lambda j, i: (0, j)),
            pl.BlockSpec((W, tn), lambda j, i: (1, j)),
            pl.BlockSpec((CONV_CH, tn), lambda j, i: (2 * W // CONV_CH, j)),
            pl.BlockSpec((tm, tn), lambda j, i: (i, j)),
        ],
        out_specs=pl.BlockSpec((tm, tn), lambda j, i: (i, j)),
        out_shape=jax.ShapeDtypeStruct((M, D), F32),
        compiler_params=_cparams(("parallel", "parallel")),
        name="outproj",
    )(fox_o, gdn_o, conf_o, w_out, w_out, w_out, h)


def _ffn_kernel(h_ref, g_ref, wg_ref, wu_ref, wd_ref, o_ref, xn_ref):
    f = pl.program_id(1)

    @pl.when(f == 0)
    def _():
        x = h_ref[...]
        xn_ref[...] = _rms_rows(x, g_ref[...]).astype(BF16)
        o_ref[...] = x

    xn = xn_ref[...]
    gate = jnp.dot(xn, wg_ref[...], preferred_element_type=F32)
    up = jnp.dot(xn, wu_ref[...], preferred_element_type=F32)
    act = (gate * _sigmoid(gate) * up).astype(BF16)
    o_ref[...] += jnp.dot(act, wd_ref[...], preferred_element_type=F32)


def _ffn(h, g, wg, wu, wd, *, tm, tf):
    M, D = h.shape
    F = wg.shape[1]
    return pl.pallas_call(
        _ffn_kernel,
        grid=(M // tm, F // tf),
        in_specs=[
            pl.BlockSpec((tm, D), lambda i, f: (i, 0)),
            pl.BlockSpec((1, D), lambda i, f: (0, 0)),
            pl.BlockSpec((D, tf), lambda i, f: (0, f)),
            pl.BlockSpec((D, tf), lambda i, f: (0, f)),
            pl.BlockSpec((tf, D), lambda i, f: (f, 0)),
        ],
        out_specs=pl.BlockSpec((tm, D), lambda i, f: (i, 0)),
        out_shape=jax.ShapeDtypeStruct((M, D), F32),
        scratch_shapes=[pltpu.VMEM((tm, D), BF16)],
        compiler_params=_cparams(("parallel", "arbitrary")),
        name="ffn",
    )(h, g, wg, wu, wd)


def _router_kernel(h_ref, g_ref, rhi_ref, rlo_ref, info_ref, cnt_ref, carry_ref, *, tm):
    i = pl.program_id(0)

    @pl.when(i == 0)
    def _():
        carry_ref[...] = jnp.zeros_like(carry_ref)

    xn = _rms_rows(h_ref[...], g_ref[...])
    x_hi = xn.astype(BF16)
    x_lo = (xn - x_hi.astype(F32)).astype(BF16)
    logits = (jnp.dot(x_hi, rhi_ref[...], preferred_element_type=F32)
              + jnp.dot(x_lo, rhi_ref[...], preferred_element_type=F32)
              + jnp.dot(x_hi, rlo_ref[...], preferred_element_type=F32))
    lane = lax.broadcasted_iota(jnp.int32, (tm, LANES), 1).astype(F32)
    logits = jnp.where(lane < N_EXPERTS, logits, -BIG)
    m1 = jnp.max(logits, axis=1, keepdims=True)
    i1 = jnp.min(jnp.where(logits == m1, lane, float(LANES)), axis=1, keepdims=True)
    rest = jnp.where(lane == i1, -BIG, logits)
    m2 = jnp.max(rest, axis=1, keepdims=True)
    i2 = jnp.min(jnp.where(rest == m2, lane, float(LANES)), axis=1, keepdims=True)
    e = jnp.exp(m2 - m1)
    g1 = 1.0 / (1.0 + e)
    g2 = e / (1.0 + e)

    sel = ((lane == i1) | (lane == i2)).astype(F32)
    row = lax.broadcasted_iota(jnp.int32, (tm, tm), 0)
    col = lax.broadcasted_iota(jnp.int32, (tm, tm), 1)
    before = jnp.dot((col < row).astype(BF16), sel.astype(BF16), preferred_element_type=F32)
    ranks = before + carry_ref[...]
    r1 = jnp.sum(jnp.where(lane == i1, ranks, 0.0), axis=1, keepdims=True)
    r2 = jnp.sum(jnp.where(lane == i2, ranks, 0.0), axis=1, keepdims=True)
    carry_ref[...] += jnp.sum(sel, axis=0, keepdims=True)
    cnt_ref[...] = carry_ref[...]

    info = jnp.where(lane == 0, i1, 0.0)
    info = jnp.where(lane == 1, i2, info)
    info = jnp.where(lane == 2, g1, info)
    info = jnp.where(lane == 3, g2, info)
    info = jnp.where(lane == 4, r1, info)
    info = jnp.where(lane == 5, r2, info)
    info_ref[...] = info


def _router(h, g, r_hi, r_lo, *, tm):
    M, D = h.shape
    return pl.pallas_call(
        functools.partial(_router_kernel, tm=tm),
        grid=(M // tm,),
        in_specs=[
            pl.BlockSpec((tm, D), lambda i: (i, 0)),
            pl.BlockSpec((1, D), lambda i: (0, 0)),
            pl.BlockSpec((D, LANES), lambda i: (0, 0)),
            pl.BlockSpec((D, LANES), lambda i: (0, 0)),
        ],
        out_specs=[
            pl.BlockSpec((tm, LANES), lambda i: (i, 0)),
            pl.BlockSpec((1, LANES), lambda i: (0, 0)),
        ],
        out_shape=[jax.ShapeDtypeStruct((M, LANES), F32), jax.ShapeDtypeStruct((1, LANES), F32)],
        scratch_shapes=[pltpu.VMEM((1, LANES), F32)],
        compiler_params=_cparams(("arbitrary",)),
        name="router",
    )(h, g, r_hi, r_lo)


PACK_SHIFT = 15
PACK = 1 << PACK_SHIFT


def _moe_kernel(te_ref, nv_ref, pk_ref, h_hbm, g_ref, wg_ref, wu_ref, wd_ref, y_hbm,
                xg, xn, acc, sem_g, sem_s, *, tm, nf):
    i = pl.program_id(0)
    f = pl.program_id(1)
    nv = nv_ref[0]
    valid = i < nv

    def start_gather(tile):
        def body(r, c):
            tok = pk_ref[tile * tm + r] & (PACK - 1)
            pltpu.make_async_copy(h_hbm.at[pl.ds(tok, 1), :], xg.at[pl.ds(r, 1), :], sem_g).start()
            return c
        lax.fori_loop(0, tm, body, 0, unroll=8)

    def start_scatter():
        def body(r, c):
            dst = pk_ref[i * tm + r] >> PACK_SHIFT
            pltpu.make_async_copy(acc.at[pl.ds(r, 1), :], y_hbm.at[pl.ds(dst, 1), :], sem_s).start()
            return c
        lax.fori_loop(0, tm, body, 0, unroll=8)

    def wait_gather():
        pltpu.make_async_copy(h_hbm.at[pl.ds(0, tm), :], xg, sem_g).wait()

    def wait_scatter():
        pltpu.make_async_copy(acc, y_hbm.at[pl.ds(0, tm), :], sem_s).wait()

    @pl.when((i == 0) & (f == 0))
    def _():
        acc[...] = jnp.zeros_like(acc)
        spare = pltpu.make_async_copy(acc, y_hbm.at[pl.ds(y_hbm.shape[0] - tm, tm), :], sem_s)
        spare.start()
        spare.wait()

    @pl.when(valid & (f == 0))
    def _():
        @pl.when(i == 0)
        def _():
            start_gather(i)

        wait_gather()
        xn[...] = _rms_rows(xg[...], g_ref[...]).astype(BF16)

        @pl.when(i + 1 < nv)
        def _():
            start_gather(i + 1)

    def hidden():
        x = xn[...]
        gate = jnp.dot(x, wg_ref[0], preferred_element_type=F32)
        up = jnp.dot(x, wu_ref[0], preferred_element_type=F32)
        return (gate * _sigmoid(gate) * up).astype(BF16)

    @pl.when(valid & (f == 0))
    def _():
        act = hidden()

        @pl.when(i > 0)
        def _():
            wait_scatter()

        acc[...] = jnp.dot(act, wd_ref[0], preferred_element_type=F32)

    @pl.when(valid & (f > 0))
    def _():
        acc[...] += jnp.dot(hidden(), wd_ref[0], preferred_element_type=F32)

    @pl.when(valid & (f == nf - 1))
    def _():
        start_scatter()

        @pl.when(i == nv - 1)
        def _():
            wait_scatter()


def _moe(tile_expert, n_valid, packed, h, g, wg, wu, wd, *, tm, tf, n_tiles):
    M, D = h.shape
    F = wg.shape[2]
    nf = F // tf

    def fblk(i, f, nv):
        return jnp.where(i < nv[0], f, nf - 1)

    return pl.pallas_call(
        functools.partial(_moe_kernel, tm=tm, nf=nf),
        grid_spec=pltpu.PrefetchScalarGridSpec(
            num_scalar_prefetch=3,
            grid=(n_tiles, nf),
            in_specs=[
                pl.BlockSpec(memory_space=pl.ANY),
                pl.BlockSpec((1, D), lambda i, f, te, nv, pk: (0, 0)),
                pl.BlockSpec((1, D, tf), lambda i, f, te, nv, pk: (te[i], 0, fblk(i, f, nv))),
                pl.BlockSpec((1, D, tf), lambda i, f, te, nv, pk: (te[i], 0, fblk(i, f, nv))),
                pl.BlockSpec((1, tf, D), lambda i, f, te, nv, pk: (te[i], fblk(i, f, nv), 0)),
            ],
            out_specs=pl.BlockSpec(memory_space=pl.ANY),
            scratch_shapes=[
                pltpu.VMEM((tm, D), F32), pltpu.VMEM((tm, D), BF16), pltpu.VMEM((tm, D), F32),
                pltpu.SemaphoreType.DMA, pltpu.SemaphoreType.DMA,
            ],
        ),
        out_shape=jax.ShapeDtypeStruct((2 * M + tm, D), F32),
        compiler_params=_cparams(("arbitrary", "arbitrary")),
        name="moe",
    )(tile_expert, n_valid, packed, h, g, wg, wu, wd)


def _combine_kernel(h_ref, y0_ref, y1_ref, info_ref, gf_ref, o_ref):
    info = info_ref[...]
    hn = h_ref[...] + info[:, 2:3] * y0_ref[...] + info[:, 3:4] * y1_ref[...]
    o_ref[...] = _rms_rows(hn, gf_ref[...])


def _combine(h, y, info, gf, *, batch, seq_len):
    M, D = h.shape
    tb = FRONT
    per_seq = seq_len // tb
    real = per_seq - 1

    def src(off):
        return lambda b, j: (off + b * per_seq + 1 + j, 0)

    return pl.pallas_call(
        _combine_kernel,
        grid=(batch, real),
        in_specs=[
            pl.BlockSpec((tb, D), src(0)),
            pl.BlockSpec((tb, D), src(0)),
            pl.BlockSpec((tb, D), src(M // tb)),
            pl.BlockSpec((tb, LANES), src(0)),
            pl.BlockSpec((1, D), lambda b, j: (0, 0)),
        ],
        out_specs=pl.BlockSpec((tb, D), lambda b, j: (b * real + j, 0)),
        out_shape=jax.ShapeDtypeStruct((batch * real * tb, D), F32),
        compiler_params=_cparams(("parallel", "parallel")),
        name="combine",
    )(h, y, y, info, gf)


def _prep_w_in(w):
    W = N_HEADS * HEAD_DIM
    sizes = [W, W, W, N_HEADS, W, W, W, N_HEADS, N_HEADS, W, 2 * CONV_CH]
    parts, o = [], 0
    for s in sizes:
        parts.append(w[:, o:o + s])
        o += s
    fq, fk, fv, ff, gq, gk, gv, ga, gb, gg, cc = parts
    main = jnp.concatenate([cc, fq * (HEAD_DIM ** -0.5 * LOG2E), fk, fv, gq, gk, gv, gg], axis=1).astype(BF16)
    small = jnp.concatenate([ff, ga, gb], axis=1)
    small = jnp.pad(small, ((0, 0), (0, LANES - small.shape[1]))).astype(BF16)
    return main, small


def _lane_vec(pairs):
    v = jnp.zeros((1, LANES), F32)
    for lane0, vals in pairs:
        v = lax.dynamic_update_slice(v, vals.astype(F32)[None, :], (0, lane0))
    return v


def _mixer(h, layer, p, *, B, L, T, tm):
    zb, zs = _inproj(h, p["attn_norm_g"][layer][None, :], *p["w_in"][layer], seq_len=L, tm=tm, tn=1280)
    zb = zb.reshape(B, L, N_MAIN)
    bias_vec = _lane_vec([(LANE_FF, p["fox_b_f"][layer]), (LANE_GA, p["gdn_dt_bias"][layer])])
    alog_vec = _lane_vec([(LANE_GA, p["gdn_A_log"][layer])])
    gates = _gates(zs.reshape(B, L, LANES), bias_vec, alog_vec, T=T)
    gates_t = jnp.swapaxes(gates[:, :, :24], 1, 2)
    grow = gates_t.reshape(B, 24, L // GDN_CHUNK, GDN_CHUNK)
    fox_o = _fox(zb, gates, T=T)
    conv_w = p["gdn_conv_w"][layer].reshape(GDN_CONV, 3 * N_HEADS, HEAD_DIM).transpose(1, 0, 2)
    gdn_o = _gdn(zb, gates, grow, conv_w, p["gdn_norm_g"][layer][None, :], T=T)
    conf_o = _conf(zb, p["cc_dw_w"][layer], p["cc_dw_b"][layer][None, :], p["cc_ln_g"][layer][None, :],
                   p["cc_ln_b"][layer][None, :], T=T)
    M = B * L
    return _outproj(fox_o.reshape(M, -1), gdn_o.reshape(M, -1), conf_o.reshape(M, -1),
                    p["w_out"][layer], h, tm=tm, tn=1024)


def _moe_layer(h, i, p, *, B, L, tm_r, tm_e, tf_e):
    M, D = h.shape
    g = p["ffn_norm_g_moe"][i][None, :]
    router = jnp.pad(p["moe_router"][i], ((0, 0), (0, LANES - N_EXPERTS)))
    r_hi = router.astype(BF16)
    r_lo = (router - r_hi.astype(F32)).astype(BF16)
    info, cnt = _router(h, g, r_hi, r_lo, tm=tm_r)

    e = info[:, 0:2].astype(jnp.int32)
    rank = info[:, 4:6].astype(jnp.int32)
    counts = cnt[0, :N_EXPERTS].astype(jnp.int32)
    padded = ((counts + tm_e - 1) // tm_e) * tm_e
    ends = jnp.cumsum(padded)
    offs = ends - padded
    pos = offs[e] + rank
    n_tiles = (2 * M + N_EXPERTS * (tm_e - 1)) // tm_e
    P = n_tiles * tm_e
    tok = jnp.arange(M, dtype=jnp.int32)[:, None]
    dst = jnp.arange(2, dtype=jnp.int32)[None, :] * M + tok
    word = dst * PACK + tok
    filler = (2 * M + jnp.arange(P, dtype=jnp.int32) % tm_e) * PACK
    packed = filler.at[pos.reshape(-1)].set(word.reshape(-1))
    n_valid = (ends[-1] // tm_e).astype(jnp.int32)
    tile_start = jnp.minimum(jnp.arange(n_tiles, dtype=jnp.int32), n_valid - 1) * tm_e
    tile_expert = jnp.minimum(jnp.searchsorted(ends, tile_start, side="right"),
                              N_EXPERTS - 1).astype(jnp.int32)

    y = _moe(tile_expert, n_valid[None], packed, h, g, p["moe_w_gate"][i], p["moe_w_up"][i],
             p["moe_w_down"][i], tm=tm_e, tf=tf_e, n_tiles=n_tiles)
    return _combine(h, y, info, p["final_norm_g"][None, :], batch=B, seq_len=L)


def kernel(x, meta_tokens, attn_norm_g, w_in, fox_b_f, gdn_conv_w, gdn_A_log, gdn_dt_bias, gdn_norm_g,
           cc_dw_w, cc_dw_b, cc_ln_g, cc_ln_b, w_out, ffn_norm_g, dense_w_gate, dense_w_up,
           dense_w_down, moe_router, moe_w_gate, moe_w_up, moe_w_down, final_norm_g):
    B, S, D = x.shape
    depth = w_in.shape[0]
    assert depth == 2, "layer 0 is dense SwiGLU, layer 1 is the routed expert SwiGLU followed by the final norm"
    L = S + FRONT
    T = 640 if L % 640 == 0 else 128
    tm = T
    p = dict(
        attn_norm_g=attn_norm_g, fox_b_f=fox_b_f, gdn_conv_w=gdn_conv_w, gdn_A_log=gdn_A_log,
        gdn_dt_bias=gdn_dt_bias, gdn_norm_g=gdn_norm_g, cc_dw_w=cc_dw_w, cc_dw_b=cc_dw_b,
        cc_ln_g=cc_ln_g, cc_ln_b=cc_ln_b, w_out=w_out.astype(BF16),
        w_in=[_prep_w_in(w_in[l]) for l in range(depth)],
        ffn_norm_g_moe=ffn_norm_g[1::2], moe_router=moe_router,
        moe_w_gate=moe_w_gate.astype(BF16), moe_w_up=moe_w_up.astype(BF16),
        moe_w_down=moe_w_down.astype(BF16), final_norm_g=final_norm_g,
    )
    pad = jnp.zeros((B, N_PAD, D), x.dtype)
    meta = jnp.broadcast_to(meta_tokens.astype(x.dtype)[None], (B, N_META, D))
    h = jnp.concatenate([pad, meta, x], axis=1).reshape(B * L, D)

    h = _mixer(h, 0, p, B=B, L=L, T=T, tm=tm)
    h = _ffn(h, ffn_norm_g[0][None, :], dense_w_gate[0].astype(BF16), dense_w_up[0].astype(BF16),
             dense_w_down[0].astype(BF16), tm=tm, tf=512)
    h = _mixer(h, 1, p, B=B, L=L, T=T, tm=tm)
    out = _moe_layer(h, 0, p, B=B, L=L, tm_r=tm, tm_e=1024, tf_e=512)
    return out.reshape(B, S, D)
```

```python
import functools

import jax
import jax.numpy as jnp
from jax import lax
from jax.experimental import pallas as pl
from jax.experimental.pallas import tpu as pltpu

F32 = jnp.float32
BF16 = jnp.bfloat16

HEAD_DIM = 128
N_HEADS = 6
CONV_CH = 512
GDN_CONV = 4
CONF_KERNEL = 31
GDN_CHUNK = 64
N_META = 16
FRONT = 128
N_PAD = FRONT - N_META
N_EXPERTS = 8
EPS = 1e-6
BIG = 1e30
LANES = 128
CONF_HALO = 32
GDN_HALO = 16
VMEM_LIMIT = 56 * 1024 * 1024

U_CCA, U_CCB, U_FQ, U_FK, U_FV, U_GQ, U_GK, U_GV, U_GG = 0, 4, 8, 14, 20, 26, 32, 38, 44
N_MAIN = 50 * LANES
LANE_FF, LANE_GA, LANE_GB = 0, 6, 12


def _cparams(sem):
    return pltpu.CompilerParams(dimension_semantics=sem, vmem_limit_bytes=VMEM_LIMIT)


def _sigmoid(x):
    return 1.0 / (1.0 + jnp.exp(-x))


def _dot(a, b):
    return jnp.dot(a.astype(BF16), b.astype(BF16), preferred_element_type=F32)


def _dot_nt(a, b):
    return lax.dot_general(a.astype(BF16), b.astype(BF16), (((1,), (1,)), ((), ())),
                           preferred_element_type=F32)


def _dot_tn(a, b):
    return lax.dot_general(a.astype(BF16), b.astype(BF16), (((0,), (0,)), ((), ())),
                           preferred_element_type=F32)


def _split3(x):
    hi = x.astype(BF16)
    r1 = x - hi.astype(F32)
    mid = r1.astype(BF16)
    lo = (r1 - mid.astype(F32)).astype(BF16)
    return hi, mid, lo


def _rms_rows(x, g):
    ms = jnp.mean(x * x, axis=-1, keepdims=True)
    return x * lax.rsqrt(ms + EPS) * g


def _inproj_kernel(h_ref, g_ref, w_ref, ws_ref, z_ref, zs_ref, xn_ref, *, tm, tiles_per_seq):
    i = pl.program_id(0)
    j = pl.program_id(1)

    @pl.when(j == 0)
    def _():
        y = _rms_rows(h_ref[...], g_ref[...])
        pos = (i % tiles_per_seq) * tm + lax.broadcasted_iota(jnp.int32, (tm, 1), 0)
        xb = jnp.where(pos >= N_PAD, y, 0.0).astype(BF16)
        xn_ref[...] = xb
        zs_ref[...] = jnp.dot(xb, ws_ref[...], preferred_element_type=F32)

    z_ref[...] = jnp.dot(xn_ref[...], w_ref[...], preferred_element_type=F32).astype(z_ref.dtype)


def _inproj(h, g, w_main, w_small, *, seq_len, tm, tn):
    M, D = h.shape
    N = w_main.shape[1]
    kern = functools.partial(_inproj_kernel, tm=tm, tiles_per_seq=seq_len // tm)
    return pl.pallas_call(
        kern,
        grid=(M // tm, N // tn),
        in_specs=[
            pl.BlockSpec((tm, D), lambda i, j: (i, 0)),
            pl.BlockSpec((1, D), lambda i, j: (0, 0)),
            pl.BlockSpec((D, tn), lambda i, j: (0, j)),
            pl.BlockSpec((D, LANES), lambda i, j: (0, 0)),
        ],
        out_specs=[
            pl.BlockSpec((tm, tn), lambda i, j: (i, j)),
            pl.BlockSpec((tm, LANES), lambda i, j: (i, 0)),
        ],
        out_shape=[jax.ShapeDtypeStruct((M, N), BF16), jax.ShapeDtypeStruct((M, LANES), F32)],
        scratch_shapes=[pltpu.VMEM((tm, D), BF16)],
        compiler_params=_cparams(("parallel", "arbitrary")),
        name="inproj",
    )(h, g, w_main, w_small)


def _gates_kernel(zs_ref, bias_ref, alog_ref, out_ref, carry_ref, *, T):
    t = pl.program_id(1)

    @pl.when(t == 0)
    def _():
        carry_ref[...] = jnp.zeros_like(carry_ref)

    x = zs_ref[0] + bias_ref[...]
    lane = lax.broadcasted_iota(jnp.int32, (T, LANES), 1)
    tail = jnp.log(1.0 + jnp.exp(-jnp.abs(x)))
    log_sig = jnp.minimum(x, 0.0) - tail
    softplus = jnp.maximum(x, 0.0) + tail
    is_f = lane < LANE_GA
    is_g = (lane >= LANE_GA) & (lane < LANE_GB)
    is_b = (lane >= LANE_GB) & (lane < LANE_GB + N_HEADS)
    val_f = jnp.where(is_f, log_sig, 0.0)
    val_g = jnp.where(is_g, -jnp.exp(alog_ref[...]) * softplus, 0.0)

    row = lax.broadcasted_iota(jnp.int32, (T, T), 0)
    col = lax.broadcasted_iota(jnp.int32, (T, T), 1)
    tri_full = (col <= row).astype(BF16)
    tri_chunk = ((col <= row) & ((col // GDN_CHUNK) == (row // GDN_CHUNK))).astype(BF16)

    def cumsum(tri, v):
        hi, mid, lo = _split3(v)
        return (jnp.dot(tri, hi, preferred_element_type=F32)
                + jnp.dot(tri, mid, preferred_element_type=F32)
                + jnp.dot(tri, lo, preferred_element_type=F32))

    c = cumsum(tri_full, val_f) + carry_ref[...]
    carry_ref[...] = c[T - 1:T, :]
    gcum = cumsum(tri_chunk, val_g)
    out_ref[0] = jnp.where(is_f, c, jnp.where(is_g, gcum, jnp.where(is_b, _sigmoid(x), 0.0)))


def _gates(zs, bias_vec, alog_vec, *, T):
    B, L, _ = zs.shape
    return pl.pallas_call(
        functools.partial(_gates_kernel, T=T),
        grid=(B, L // T),
        in_specs=[
            pl.BlockSpec((1, T, LANES), lambda b, t: (b, t, 0)),
            pl.BlockSpec((1, LANES), lambda b, t: (0, 0)),
            pl.BlockSpec((1, LANES), lambda b, t: (0, 0)),
        ],
        out_specs=pl.BlockSpec((1, T, LANES), lambda b, t: (b, t, 0)),
        out_shape=jax.ShapeDtypeStruct((B, L, LANES), F32),
        scratch_shapes=[pltpu.VMEM((1, LANES), F32)],
        compiler_params=_cparams(("parallel", "arbitrary")),
        name="gates",
    )(zs, bias_vec, alog_vec)


LOG2E = 1.4426950408889634
AUG_STRIDE = 8


def _fox_kernel(q_ref, k_ref, v_ref, g_ref, o_ref, kaug, s0, s1, m_s, l_s, acc_s, *, T):
    h = pl.program_id(1)
    qi = pl.program_id(2)
    L = k_ref.shape[1]

    @pl.when(qi == 0)
    def _():
        def build(j, c):
            r0 = pl.multiple_of(j * T, T)
            pos = r0 + lax.broadcasted_iota(jnp.int32, (T, LANES), 0)
            lane = lax.broadcasted_iota(jnp.int32, (T, LANES), 1)
            ck = jnp.where(pos < N_PAD, BIG, g_ref[0, pl.ds(r0, T), :] * LOG2E)
            hi, mid, lo = (x.astype(F32) for x in _split3(ck))
            ext = jnp.where(lane < AUG_STRIDE, hi,
                            jnp.where(lane < 2 * AUG_STRIDE, pltpu.roll(mid, AUG_STRIDE, 1),
                                      jnp.where(lane < 3 * AUG_STRIDE, pltpu.roll(lo, 2 * AUG_STRIDE, 1), 0.0)))
            kaug[pl.ds(r0, T), 0:HEAD_DIM] = k_ref[0, pl.ds(r0, T), :]
            kaug[pl.ds(r0, T), HEAD_DIM:2 * HEAD_DIM] = ext.astype(BF16)
            return c
        lax.fori_loop(0, L // T, build, 0)

    lane = lax.broadcasted_iota(jnp.int32, (T, LANES), 1)
    pick = (lane == h) | (lane == h + AUG_STRIDE) | (lane == h + 2 * AUG_STRIDE)
    q = jnp.concatenate([q_ref[0], jnp.where(pick, -1.0, 0.0).astype(BF16)], axis=1)

    def logits(kb, s_ref):
        start = pl.multiple_of(kb * T, T)
        s_ref[...] = lax.dot_general(q, kaug[pl.ds(start, T), :], (((1,), (1,)), ((), ())),
                                     preferred_element_type=F32)

    def softmax_pv(kb, s_ref, carry, masked=False):
        m, l, acc = carry
        start = pl.multiple_of(kb * T, T)
        s = s_ref[...]
        if masked:
            row = lax.broadcasted_iota(jnp.int32, (T, T), 0)
            col = lax.broadcasted_iota(jnp.int32, (T, T), 1)
            s = jnp.where(col <= row, s, -BIG)
        m_new = jnp.maximum(m, jnp.max(s, axis=1, keepdims=True))
        alpha = jnp.exp2(m - m_new)
        p = jnp.exp2(s - m_new)
        l = alpha * l + jnp.sum(p, axis=1, keepdims=True)
        acc = alpha * acc + jnp.dot(p.astype(BF16), v_ref[0, pl.ds(start, T), :], preferred_element_type=F32)
        return m_new, l, acc

    def finish(carry):
        m, l, acc = carry
        o_ref[0] = (acc / l).astype(o_ref.dtype)

    def pair(j, carry):
        logits(2 * j + 1, s1)
        carry = softmax_pv(2 * j, s0, carry)
        logits(2 * j + 2, s0)
        return softmax_pv(2 * j + 1, s1, carry)

    init = (jnp.full((T, 1), -3.0e38, F32), jnp.zeros((T, 1), F32), jnp.zeros((T, HEAD_DIM), F32))
    logits(0, s0)
    carry = lax.fori_loop(0, qi // 2, pair, init)
    m_s[...], l_s[...], acc_s[...] = carry

    @pl.when(qi % 2 == 0)
    def _():
        finish(softmax_pv(qi, s0, (m_s[...], l_s[...], acc_s[...]), masked=True))

    @pl.when(qi % 2 == 1)
    def _():
        logits(qi, s1)
        carry = softmax_pv(qi - 1, s0, (m_s[...], l_s[...], acc_s[...]))
        finish(softmax_pv(qi, s1, carry, masked=True))


def _fox(zb, gates, *, T):
    B, L, _ = zb.shape
    nblk = L // T
    assert LANE_FF + N_HEADS <= AUG_STRIDE
    return pl.pallas_call(
        functools.partial(_fox_kernel, T=T),
        grid=(B, N_HEADS, nblk),
        in_specs=[
            pl.BlockSpec((1, T, HEAD_DIM), lambda b, h, i: (b, i, U_FQ + h)),
            pl.BlockSpec((1, L, HEAD_DIM), lambda b, h, i: (b, 0, U_FK + h)),
            pl.BlockSpec((1, L, HEAD_DIM), lambda b, h, i: (b, 0, U_FV + h)),
            pl.BlockSpec((1, L, LANES), lambda b, h, i: (b, 0, 0)),
        ],
        out_specs=pl.BlockSpec((1, T, HEAD_DIM), lambda b, h, i: (b, i, h)),
        out_shape=jax.ShapeDtypeStruct((B, L, N_HEADS * HEAD_DIM), BF16),
        scratch_shapes=[pltpu.VMEM((L, 2 * HEAD_DIM), BF16), pltpu.VMEM((T, T), F32), pltpu.VMEM((T, T), F32),
                        pltpu.VMEM((T, 1), F32), pltpu.VMEM((T, 1), F32), pltpu.VMEM((T, HEAD_DIM), F32)],
        compiler_params=_cparams(("parallel", "parallel", "arbitrary")),
        name="fox",
    )(zb, zb, zb, gates)


def _gdn_kernel(q_ref, k_ref, v_ref, qh_ref, kh_ref, vh_ref, wq_ref, wk_ref, wv_ref, gt_ref, gr_ref,
                gate_ref, ng_ref, o_ref, S_ref, buf, *, T, HG):
    hp = pl.program_id(1)
    t = pl.program_id(2)
    C = GDN_CHUNK
    nchunk = T // C

    @pl.when(t == 0)
    def _():
        S_ref[...] = jnp.zeros_like(S_ref)

    def conv_silu(x_ref, halo_ref, w_ref, j):
        lanes = slice(j * HEAD_DIM, (j + 1) * HEAD_DIM)
        halo = jnp.where(t > 0, halo_ref[0, :, lanes].astype(F32), 0.0)
        buf[0:GDN_HALO, :] = halo
        buf[GDN_HALO:GDN_HALO + T, :] = x_ref[0, :, lanes].astype(F32)
        w = w_ref[j]
        y = w[0:1, :] * buf[pl.ds(GDN_HALO - GDN_CONV + 1, T), :]
        for i in range(1, GDN_CONV):
            y = y + w[i:i + 1, :] * buf[pl.ds(GDN_HALO - GDN_CONV + 1 + i, T), :]
        return y * _sigmoid(y)

    def l2n(x):
        return x * lax.rsqrt(jnp.sum(x * x, axis=-1, keepdims=True) + EPS)

    gt = gt_ref[0]
    lane = lax.broadcasted_iota(jnp.int32, (T, LANES), 1)
    q, k, g_b, kb, vb, kg, qd = [], [], [], [], [], [], []
    for j in range(HG):
        h = hp * HG + j
        qj = l2n(conv_silu(q_ref, qh_ref, wq_ref, j)) * (HEAD_DIM ** -0.5)
        kj = l2n(conv_silu(k_ref, kh_ref, wk_ref, j))
        vj = conv_silu(v_ref, vh_ref, wv_ref, j)
        g_col = jnp.sum(jnp.where(lane == LANE_GA + h, gt, 0.0), axis=1, keepdims=True)
        beta = jnp.sum(jnp.where(lane == LANE_GB + h, gt, 0.0), axis=1, keepdims=True)
        gj = jnp.broadcast_to(g_col, (T, HEAD_DIM))
        eg = jnp.exp(gj)
        q.append(qj)
        k.append(kj)
        g_b.append(gj)
        kb.append(kj * beta)
        vb.append(vj * beta)
        kg.append(kj * beta * eg)
        qd.append(qj * eg)

    row = lax.broadcasted_iota(jnp.int32, (C, C), 0)
    col = lax.broadcasted_iota(jnp.int32, (C, C), 1)
    incl = col <= row
    strict = col < row
    eye = (col == row).astype(F32)

    ps = [(j, c) for c in range(nchunk) for j in range(HG)]
    rows = lambda x, jc: x[jc[0]][jc[1] * C:(jc[1] + 1) * C]
    g_last = {jc: g_b[jc[0]][(jc[1] + 1) * C - 1:(jc[1] + 1) * C] for jc in ps}
    decay = {}
    for jc in ps:
        g_row = gr_ref[0, LANE_GA + hp * HG + jc[0], pl.ds(t * nchunk + jc[1], 1), :]
        diff = rows(g_b, jc)[:, 0:C] - g_row
        decay[jc] = jnp.where(incl, jnp.exp(jnp.where(incl, diff, 0.0)), 0.0)
    kq = {jc: _dot_nt(jnp.concatenate([rows(kb, jc), rows(q, jc)], axis=0), rows(k, jc)) for jc in ps}
    npow = {jc: jnp.where(strict, kq[jc][:C] * decay[jc], 0.0) for jc in ps}
    qk = {jc: kq[jc][C:] * decay[jc] for jc in ps}
    p = {jc: eye - npow[jc] for jc in ps}
    for _ in range(5):
        npow = {jc: _dot(npow[jc], npow[jc]) for jc in ps}
        p = {jc: p[jc] + _dot(p[jc], npow[jc]) for jc in ps}
    wu = {jc: _dot(p[jc], jnp.concatenate([rows(kg, jc), rows(vb, jc)], axis=1)) for jc in ps}
    kwu = {jc: _dot_tn(rows(k, jc) * jnp.exp(g_last[jc] - rows(g_b, jc)), wu[jc]) for jc in ps}
    qwu = {jc: _dot(qk[jc], wu[jc]) for jc in ps}
    S = [S_ref[j] for j in range(HG)]
    outs = [[] for _ in range(HG)]
    for jc in ps:
        j = jc[0]
        lhs = jnp.concatenate([kwu[jc][:, :HEAD_DIM], rows(qd, jc) - qwu[jc][:, :HEAD_DIM]], axis=0)
        xs = _dot(lhs, S[j])
        outs[j].append(xs[HEAD_DIM:] + qwu[jc][:, HEAD_DIM:])
        S[j] = S[j] * jnp.exp(g_last[jc]) - xs[:HEAD_DIM] + kwu[jc][:, HEAD_DIM:]

    for j in range(HG):
        S_ref[j] = S[j]
        lanes = slice(j * HEAD_DIM, (j + 1) * HEAD_DIM)
        o = jnp.concatenate(outs[j], axis=0)
        y = o * lax.rsqrt(jnp.mean(o * o, axis=-1, keepdims=True) + EPS) * ng_ref[...]
        gate = gate_ref[0, :, lanes].astype(F32)
        o_ref[0, :, lanes] = (y * (gate * _sigmoid(gate))).astype(o_ref.dtype)


def _gdn(zb, gates, grow, conv_w, norm_g, *, T, HG=2):
    B, L, _ = zb.shape
    nblk = L // T
    hb = T // GDN_HALO
    W = HG * HEAD_DIM

    def cur(u):
        return pl.BlockSpec((1, T, W), lambda b, h, t: (b, t, u // HG + h))

    def halo(u):
        return pl.BlockSpec((1, GDN_HALO, W),
                            lambda b, h, t: (b, jnp.maximum(t * hb - 1, 0), u // HG + h))

    def cw(j):
        return pl.BlockSpec((HG, GDN_CONV, HEAD_DIM), lambda b, h, t: (j * (N_HEADS // HG) + h, 0, 0))

    assert all(u % HG == 0 for u in (U_GQ, U_GK, U_GV, U_GG)) and N_HEADS % HG == 0
    return pl.pallas_call(
        functools.partial(_gdn_kernel, T=T, HG=HG),
        grid=(B, N_HEADS // HG, nblk),
        in_specs=[
            cur(U_GQ), cur(U_GK), cur(U_GV), halo(U_GQ), halo(U_GK), halo(U_GV),
            cw(0), cw(1), cw(2),
            pl.BlockSpec((1, T, LANES), lambda b, h, t: (b, t, 0)),
            pl.BlockSpec((1, grow.shape[1], L // GDN_CHUNK, GDN_CHUNK), lambda b, h, t: (b, 0, 0, 0)),
            cur(U_GG),
            pl.BlockSpec((1, HEAD_DIM), lambda b, h, t: (0, 0)),
        ],
        out_specs=pl.BlockSpec((1, T, W), lambda b, h, t: (b, t, h)),
        out_shape=jax.ShapeDtypeStruct((B, L, N_HEADS * HEAD_DIM), BF16),
        scratch_shapes=[pltpu.VMEM((HG, HEAD_DIM, HEAD_DIM), F32), pltpu.VMEM((T + GDN_HALO, HEAD_DIM), F32)],
        compiler_params=_cparams(("parallel", "parallel", "arbitrary")),
        name="gdn",
    )(zb, zb, zb, zb, zb, zb, conv_w, conv_w, conv_w, gates, grow, zb, norm_g)


def _conf_kernel(a_ref, b_ref, ah_ref, bh_ref, w_ref, wb_ref, lg_ref, lb_ref, o_ref, ubuf, *, T, RC):
    t = pl.program_id(1)
    ah = ah_ref[0].astype(F32)
    bh = bh_ref[0].astype(F32)
    ubuf[0:CONF_HALO, :] = jnp.where(t > 0, ah * _sigmoid(bh), 0.0)
    a = a_ref[0].astype(F32)
    b = b_ref[0].astype(F32)
    ubuf[CONF_HALO:CONF_HALO + T, :] = a * _sigmoid(b)
    base = CONF_HALO - CONF_KERNEL + 1
    for r0 in range(0, T, RC):
        acc = jnp.broadcast_to(wb_ref[...], (RC, CONV_CH))
        for i in range(CONF_KERNEL):
            acc = acc + w_ref[i:i + 1, :] * ubuf[pl.ds(r0 + base + i, RC), :]
        mu = jnp.mean(acc, axis=-1, keepdims=True)
        d = acc - mu
        var = jnp.mean(d * d, axis=-1, keepdims=True)
        un = d * lax.rsqrt(var + EPS) * lg_ref[...] + lb_ref[...]
        o_ref[0, pl.ds(r0, RC), :] = (un * _sigmoid(un)).astype(o_ref.dtype)


def _conf(zb, dw_w, dw_b, ln_g, ln_b, *, T, RC=64):
    B, L, _ = zb.shape
    hb = T // CONF_HALO
    ua, ub = U_CCA * LANES // CONV_CH, U_CCB * LANES // CONV_CH
    vec = pl.BlockSpec((1, CONV_CH), lambda b, t: (0, 0))

    def halo(u):
        return pl.BlockSpec((1, CONF_HALO, CONV_CH), lambda b, t: (b, jnp.maximum(t * hb - 1, 0), u))

    return pl.pallas_call(
        functools.partial(_conf_kernel, T=T, RC=RC),
        grid=(B, L // T),
        in_specs=[
            pl.BlockSpec((1, T, CONV_CH), lambda b, t: (b, t, ua)),
            pl.BlockSpec((1, T, CONV_CH), lambda b, t: (b, t, ub)),
            halo(ua), halo(ub),
            pl.BlockSpec((CONF_KERNEL, CONV_CH), lambda b, t: (0, 0)),
            vec, vec, vec,
        ],
        out_specs=pl.BlockSpec((1, T, CONV_CH), lambda b, t: (b, t, 0)),
        out_shape=jax.ShapeDtypeStruct((B, L, CONV_CH), BF16),
        scratch_shapes=[pltpu.VMEM((T + CONF_HALO, CONV_CH), F32)],
        compiler_params=_cparams(("parallel", "parallel")),
        name="conf",
    )(zb, zb, zb, zb, dw_w, dw_b, ln_g, ln_b)


def _outproj_kernel(fox_ref, gdn_ref, conf_ref, wf_ref, wg_ref, wc_ref, h_ref, o_ref):
    acc = jnp.dot(fox_ref[...], wf_ref[...], preferred_element_type=F32)
    acc = acc + jnp.dot(gdn_ref[...], wg_ref[...], preferred_element_type=F32)
    acc = acc + jnp.dot(conf_ref[...], wc_ref[...], preferred_element_type=F32)
    o_ref[...] = h_ref[...] + acc


def _outproj(fox_o, gdn_o, conf_o, w_out, h, *, tm, tn):
    M, D = h.shape
    W = N_HEADS * HEAD_DIM
    return pl.pallas_call(
        _outproj_kernel,
        grid=(D // tn, M // tm),
        in_specs=[
            pl.BlockSpec((tm, W), lambda j, i: (i, 0)),
            pl.BlockSpec((tm, W), lambda j, i: (i, 0)),
            pl.BlockSpec((tm, CONV_CH), lambda j, i: (i, 0)),
            pl.BlockSpec((W, tn), lambda j, i: (0, j)),
            pl.BlockSpec((W, tn), lambda j, i: (1, j)),
            pl.BlockSpec((CONV_CH, tn), lambda j, i: (2 * W // CONV_CH, j)),
            pl.BlockSpec((tm, tn), lambda j, i: (i, j)),
        ],
        out_specs=pl.BlockSpec((tm, tn), lambda j, i: (i, j)),
        out_shape=jax.ShapeDtypeStruct((M, D), F32),
        compiler_params=_cparams(("parallel", "parallel")),
        name="outproj",
    )(fox_o, gdn_o, conf_o, w_out, w_out, w_out, h)


def _ffn_kernel(h_ref, g_ref, wg_ref, wu_ref, wd_ref, o_ref, xn_ref):
    f = pl.program_id(1)

    @pl.when(f == 0)
    def _():
        x = h_ref[...]
        xn_ref[...] = _rms_rows(x, g_ref[...]).astype(BF16)
        o_ref[...] = x

    xn = xn_ref[...]
    gate = jnp.dot(xn, wg_ref[...], preferred_element_type=F32)
    up = jnp.dot(xn, wu_ref[...], preferred_element_type=F32)
    act = (gate * _sigmoid(gate) * up).astype(BF16)
    o_ref[...] += jnp.dot(act, wd_ref[...], preferred_element_type=F32)


def _ffn(h, g, wg, wu, wd, *, tm, tf):
    M, D = h.shape
    F = wg.shape[1]
    return pl.pallas_call(
        _ffn_kernel,
        grid=(M // tm, F // tf),
        in_specs=[
            pl.BlockSpec((tm, D), lambda i, f: (i, 0)),
            pl.BlockSpec((1, D), lambda i, f: (0, 0)),
            pl.BlockSpec((D, tf), lambda i, f: (0, f)),
            pl.BlockSpec((D, tf), lambda i, f: (0, f)),
            pl.BlockSpec((tf, D), lambda i, f: (f, 0)),
        ],
        out_specs=pl.BlockSpec((tm, D), lambda i, f: (i, 0)),
        out_shape=jax.ShapeDtypeStruct((M, D), F32),
        scratch_shapes=[pltpu.VMEM((tm, D), BF16)],
        compiler_params=_cparams(("parallel", "arbitrary")),
        name="ffn",
    )(h, g, wg, wu, wd)


def _router_kernel(h_ref, g_ref, rhi_ref, rlo_ref, info_ref, cnt_ref, carry_ref, *, tm):
    i = pl.program_id(0)

    @pl.when(i == 0)
    def _():
        carry_ref[...] = jnp.zeros_like(carry_ref)

    xn = _rms_rows(h_ref[...], g_ref[...])
    x_hi = xn.astype(BF16)
    x_lo = (xn - x_hi.astype(F32)).astype(BF16)
    logits = (jnp.dot(x_hi, rhi_ref[...], preferred_element_type=F32)
              + jnp.dot(x_lo, rhi_ref[...], preferred_element_type=F32)
              + jnp.dot(x_hi, rlo_ref[...], preferred_element_type=F32))
    lane = lax.broadcasted_iota(jnp.int32, (tm, LANES), 1).astype(F32)
    logits = jnp.where(lane < N_EXPERTS, logits, -BIG)
    m1 = jnp.max(logits, axis=1, keepdims=True)
    i1 = jnp.min(jnp.where(logits == m1, lane, float(LANES)), axis=1, keepdims=True)
    rest = jnp.where(lane == i1, -BIG, logits)
    m2 = jnp.max(rest, axis=1, keepdims=True)
    i2 = jnp.min(jnp.where(rest == m2, lane, float(LANES)), axis=1, keepdims=True)
    e = jnp.exp(m2 - m1)
    g1 = 1.0 / (1.0 + e)
    g2 = e / (1.0 + e)

    sel = ((lane == i1) | (lane == i2)).astype(F32)
    row = lax.broadcasted_iota(jnp.int32, (tm, tm), 0)
    col = lax.broadcasted_iota(jnp.int32, (tm, tm), 1)
    before = jnp.dot((col < row).astype(BF16), sel.astype(BF16), preferred_element_type=F32)
    ranks = before + carry_ref[...]
    r1 = jnp.sum(jnp.where(lane == i1, ranks, 0.0), axis=1, keepdims=True)
    r2 = jnp.sum(jnp.where(lane == i2, ranks, 0.0), axis=1, keepdims=True)
    carry_ref[...] += jnp.sum(sel, axis=0, keepdims=True)
    cnt_ref[...] = carry_ref[...]

    info = jnp.where(lane == 0, i1, 0.0)
    info = jnp.where(lane == 1, i2, info)
    info = jnp.where(lane == 2, g1, info)
    info = jnp.where(lane == 3, g2, info)
    info = jnp.where(lane == 4, r1, info)
    info = jnp.where(lane == 5, r2, info)
    info_ref[...] = info


def _router(h, g, r_hi, r_lo, *, tm):
    M, D = h.shape
    return pl.pallas_call(
        functools.partial(_router_kernel, tm=tm),
        grid=(M // tm,),
        in_specs=[
            pl.BlockSpec((tm, D), lambda i: (i, 0)),
            pl.BlockSpec((1, D), lambda i: (0, 0)),
            pl.BlockSpec((D, LANES), lambda i: (0, 0)),
            pl.BlockSpec((D, LANES), lambda i: (0, 0)),
        ],
        out_specs=[
            pl.BlockSpec((tm, LANES), lambda i: (i, 0)),
            pl.BlockSpec((1, LANES), lambda i: (0, 0)),
        ],
        out_shape=[jax.ShapeDtypeStruct((M, LANES), F32), jax.ShapeDtypeStruct((1, LANES), F32)],
        scratch_shapes=[pltpu.VMEM((1, LANES), F32)],
        compiler_params=_cparams(("arbitrary",)),
        name="router",
    )(h, g, r_hi, r_lo)


PACK_SHIFT = 15
PACK = 1 << PACK_SHIFT


def _moe_kernel(te_ref, nv_ref, pk_ref, h_hbm, g_ref, wg_ref, wu_ref, wd_ref, y_hbm,
                xg, xn, acc, sem_g, sem_s, *, tm, nf):
    i = pl.program_id(0)
    f = pl.program_id(1)
    nv = nv_ref[0]
    valid = i < nv

    def start_gather(tile):
        def body(r, c):
            tok = pk_ref[tile * tm + r] & (PACK - 1)
            pltpu.make_async_copy(h_hbm.at[pl.ds(tok, 1), :], xg.at[pl.ds(r, 1), :], sem_g).start()
            return c
        lax.fori_loop(0, tm, body, 0, unroll=8)

    def start_scatter():
        def body(r, c):
            dst = pk_ref[i * tm + r] >> PACK_SHIFT
            pltpu.make_async_copy(acc.at[pl.ds(r, 1), :], y_hbm.at[pl.ds(dst, 1), :], sem_s).start()
            return c
        lax.fori_loop(0, tm, body, 0, unroll=8)

    def wait_gather():
        pltpu.make_async_copy(h_hbm.at[pl.ds(0, tm), :], xg, sem_g).wait()

    def wait_scatter():
        pltpu.make_async_copy(acc, y_hbm.at[pl.ds(0, tm), :], sem_s).wait()

    @pl.when((i == 0) & (f == 0))
    def _():
        acc[...] = jnp.zeros_like(acc)
        spare = pltpu.make_async_copy(acc, y_hbm.at[pl.ds(y_hbm.shape[0] - tm, tm), :], sem_s)
        spare.start()
        spare.wait()

    @pl.when(valid & (f == 0))
    def _():
        @pl.when(i == 0)
        def _():
            start_gather(i)

        wait_gather()
        xn[...] = _rms_rows(xg[...], g_ref[...]).astype(BF16)

        @pl.when(i + 1 < nv)
        def _():
            start_gather(i + 1)

    def hidden():
        x = xn[...]
        gate = jnp.dot(x, wg_ref[0].astype(BF16), preferred_element_type=F32)
        up = jnp.dot(x, wu_ref[0].astype(BF16), preferred_element_type=F32)
        return (gate * _sigmoid(gate) * up).astype(BF16)

    @pl.when(valid & (f == 0))
    def _():
        act = hidden()

        @pl.when(i > 0)
        def _():
            wait_scatter()

        acc[...] = jnp.dot(act, wd_ref[0].astype(BF16), preferred_element_type=F32)

    @pl.when(valid & (f > 0))
    def _():
        acc[...] += jnp.dot(hidden(), wd_ref[0].astype(BF16), preferred_element_type=F32)

    @pl.when(valid & (f == nf - 1))
    def _():
        start_scatter()

        @pl.when(i == nv - 1)
        def _():
            wait_scatter()


def _moe(tile_expert, n_valid, packed, h, g, wg, wu, wd, *, tm, tf, n_tiles):
    M, D = h.shape
    F = wg.shape[2]
    nf = F // tf

    def fblk(i, f, nv):
        return jnp.where(i < nv[0], f, nf - 1)

    return pl.pallas_call(
        functools.partial(_moe_kernel, tm=tm, nf=nf),
        grid_spec=pltpu.PrefetchScalarGridSpec(
            num_scalar_prefetch=3,
            grid=(n_tiles, nf),
            in_specs=[
                pl.BlockSpec(memory_space=pl.ANY),
                pl.BlockSpec((1, D), lambda i, f, te, nv, pk: (0, 0)),
                pl.BlockSpec((1, D, tf), lambda i, f, te, nv, pk: (te[i], 0, fblk(i, f, nv))),
                pl.BlockSpec((1, D, tf), lambda i, f, te, nv, pk: (te[i], 0, fblk(i, f, nv))),
                pl.BlockSpec((1, tf, D), lambda i, f, te, nv, pk: (te[i], fblk(i, f, nv), 0)),
            ],
            out_specs=pl.BlockSpec(memory_space=pl.ANY),
            scratch_shapes=[
                pltpu.VMEM((tm, D), F32), pltpu.VMEM((tm, D), BF16), pltpu.VMEM((tm, D), F32),
                pltpu.SemaphoreType.DMA, pltpu.SemaphoreType.DMA,
            ],
        ),
        out_shape=jax.ShapeDtypeStruct((2 * M + tm, D), F32),
        compiler_params=_cparams(("arbitrary", "arbitrary")),
        name="moe",
    )(tile_expert, n_valid, packed, h, g, wg, wu, wd)


def _combine_kernel(h_ref, y0_ref, y1_ref, info_ref, gf_ref, o_ref):
    info = info_ref[...]
    hn = h_ref[...] + info[:, 2:3] * y0_ref[...] + info[:, 3:4] * y1_ref[...]
    o_ref[...] = _rms_rows(hn, gf_ref[...])


def _combine(h, y, info, gf, *, batch, seq_len):
    M, D = h.shape
    tb = FRONT
    per_seq = seq_len // tb
    real = per_seq - 1

    def src(off):
        return lambda b, j: (off + b * per_seq + 1 + j, 0)

    return pl.pallas_call(
        _combine_kernel,
        grid=(batch, real),
        in_specs=[
            pl.BlockSpec((tb, D), src(0)),
            pl.BlockSpec((tb, D), src(0)),
            pl.BlockSpec((tb, D), src(M // tb)),
            pl.BlockSpec((tb, LANES), src(0)),
            pl.BlockSpec((1, D), lambda b, j: (0, 0)),
        ],
        out_specs=pl.BlockSpec((tb, D), lambda b, j: (b * real + j, 0)),
        out_shape=jax.ShapeDtypeStruct((batch * real * tb, D), F32),
        compiler_params=_cparams(("parallel", "parallel")),
        name="combine",
    )(h, y, y, info, gf)


def _prep_w_in(w):
    W = N_HEADS * HEAD_DIM
    sizes = [W, W, W, N_HEADS, W, W, W, N_HEADS, N_HEADS, W, 2 * CONV_CH]
    parts, o = [], 0
    for s in sizes:
        parts.append(w[:, o:o + s])
        o += s
    fq, fk, fv, ff, gq, gk, gv, ga, gb, gg, cc = parts
    main = jnp.concatenate([cc, fq * (HEAD_DIM ** -0.5 * LOG2E), fk, fv, gq, gk, gv, gg], axis=1).astype(BF16)
    small = jnp.concatenate([ff, ga, gb], axis=1)
    small = jnp.pad(small, ((0, 0), (0, LANES - small.shape[1]))).astype(BF16)
    return main, small


def _lane_vec(pairs):
    v = jnp.zeros((1, LANES), F32)
    for lane0, vals in pairs:
        v = lax.dynamic_update_slice(v, vals.astype(F32)[None, :], (0, lane0))
    return v


def _mixer(h, layer, p, *, B, L, T, tm):
    zb, zs = _inproj(h, p["attn_norm_g"][layer][None, :], *p["w_in"][layer], seq_len=L, tm=tm, tn=1280)
    zb = zb.reshape(B, L, N_MAIN)
    bias_vec = _lane_vec([(LANE_FF, p["fox_b_f"][layer]), (LANE_GA, p["gdn_dt_bias"][layer])])
    alog_vec = _lane_vec([(LANE_GA, p["gdn_A_log"][layer])])
    gates = _gates(zs.reshape(B, L, LANES), bias_vec, alog_vec, T=T)
    gates_t = jnp.swapaxes(gates[:, :, :24], 1, 2)
    grow = gates_t.reshape(B, 24, L // GDN_CHUNK, GDN_CHUNK)
    fox_o = _fox(zb, gates, T=T)
    conv_w = p["gdn_conv_w"][layer].reshape(GDN_CONV, 3 * N_HEADS, HEAD_DIM).transpose(1, 0, 2)
    gdn_o = _gdn(zb, gates, grow, conv_w, p["gdn_norm_g"][layer][None, :], T=T)
    conf_o = _conf(zb, p["cc_dw_w"][layer], p["cc_dw_b"][layer][None, :], p["cc_ln_g"][layer][None, :],
                   p["cc_ln_b"][layer][None, :], T=T)
    M = B * L
    return _outproj(fox_o.reshape(M, -1), gdn_o.reshape(M, -1), conf_o.reshape(M, -1),
                    p["w_out"][layer], h, tm=tm, tn=1024)


def _moe_layer(h, i, p, *, B, L, tm_r, tm_e, tf_e):
    M, D = h.shape
    g = p["ffn_norm_g_moe"][i][None, :]
    router = jnp.pad(p["moe_router"][i], ((0, 0), (0, LANES - N_EXPERTS)))
    r_hi = router.astype(BF16)
    r_lo = (router - r_hi.astype(F32)).astype(BF16)
    info, cnt = _router(h, g, r_hi, r_lo, tm=tm_r)

    e = info[:, 0:2].astype(jnp.int32)
    rank = info[:, 4:6].astype(jnp.int32)
    counts = cnt[0, :N_EXPERTS].astype(jnp.int32)
    padded = ((counts + tm_e - 1) // tm_e) * tm_e
    ends = jnp.cumsum(padded)
    offs = ends - padded
    pos = offs[e] + rank
    n_tiles = (2 * M + N_EXPERTS * (tm_e - 1)) // tm_e
    P = n_tiles * tm_e
    tok = jnp.arange(M, dtype=jnp.int32)[:, None]
    dst = jnp.arange(2, dtype=jnp.int32)[None, :] * M + tok
    word = dst * PACK + tok
    filler = (2 * M + jnp.arange(P, dtype=jnp.int32) % tm_e) * PACK
    packed = filler.at[pos.reshape(-1)].set(word.reshape(-1))
    n_valid = (ends[-1] // tm_e).astype(jnp.int32)
    tile_start = jnp.minimum(jnp.arange(n_tiles, dtype=jnp.int32), n_valid - 1) * tm_e
    tile_expert = jnp.minimum(jnp.searchsorted(ends, tile_start, side="right"),
                              N_EXPERTS - 1).astype(jnp.int32)

    y = _moe(tile_expert, n_valid[None], packed, h, g, p["moe_w_gate"][i], p["moe_w_up"][i],
             p["moe_w_down"][i], tm=tm_e, tf=tf_e, n_tiles=n_tiles)
    return _combine(h, y, info, p["final_norm_g"][None, :], batch=B, seq_len=L)


def kernel(x, meta_tokens, attn_norm_g, w_in, fox_b_f, gdn_conv_w, gdn_A_log, gdn_dt_bias, gdn_norm_g,
           cc_dw_w, cc_dw_b, cc_ln_g, cc_ln_b, w_out, ffn_norm_g, dense_w_gate, dense_w_up,
           dense_w_down, moe_router, moe_w_gate, moe_w_up, moe_w_down, final_norm_g):
    B, S, D = x.shape
    depth = w_in.shape[0]
    assert depth == 2, "layer 0 is dense SwiGLU, layer 1 is the routed expert SwiGLU followed by the final norm"
    L = S + FRONT
    T = 640 if L % 640 == 0 else 128
    tm = T
    p = dict(
        attn_norm_g=attn_norm_g, fox_b_f=fox_b_f, gdn_conv_w=gdn_conv_w, gdn_A_log=gdn_A_log,
        gdn_dt_bias=gdn_dt_bias, gdn_norm_g=gdn_norm_g, cc_dw_w=cc_dw_w, cc_dw_b=cc_dw_b,
        cc_ln_g=cc_ln_g, cc_ln_b=cc_ln_b, w_out=w_out.astype(BF16),
        w_in=[_prep_w_in(w_in[l]) for l in range(depth)],
        ffn_norm_g_moe=ffn_norm_g[1::2], moe_router=moe_router,
        moe_w_gate=moe_w_gate, moe_w_up=moe_w_up, moe_w_down=moe_w_down, final_norm_g=final_norm_g,
    )
    pad = jnp.zeros((B, N_PAD, D), x.dtype)
    meta = jnp.broadcast_to(meta_tokens.astype(x.dtype)[None], (B, N_META, D))
    h = jnp.concatenate([pad, meta, x], axis=1).reshape(B * L, D)

    h = _mixer(h, 0, p, B=B, L=L, T=T, tm=tm)
    h = _ffn(h, ffn_norm_g[0][None, :], dense_w_gate[0].astype(BF16), dense_w_up[0].astype(BF16),
             dense_w_down[0].astype(BF16), tm=tm, tf=512)
    h = _mixer(h, 1, p, B=B, L=L, T=T, tm=tm)
    out = _moe_layer(h, 0, p, B=B, L=L, tm_r=tm, tm_e=1024, tf_e=512)
    return out.reshape(B, S, D)
```

```python
import functools

import jax
import jax.numpy as jnp
from jax import lax
from jax.experimental import pallas as pl
from jax.experimental.pallas import tpu as pltpu

F32 = jnp.float32
BF16 = jnp.bfloat16

HEAD_DIM = 128
N_HEADS = 6
CONV_CH = 512
GDN_CONV = 4
CONF_KERNEL = 31
GDN_CHUNK = 64
N_META = 16
FRONT = 128
N_PAD = FRONT - N_META
N_EXPERTS = 8
EPS = 1e-6
BIG = 1e30
LANES = 128
CONF_HALO = 32
GDN_HALO = 16
VMEM_LIMIT = 56 * 1024 * 1024

U_CCA, U_CCB, U_FQ, U_FK, U_FV, U_GQ, U_GK, U_GV, U_GG = 0, 4, 8, 14, 20, 26, 32, 38, 44
N_MAIN = 50 * LANES
LANE_FF, LANE_GA, LANE_GB = 0, 6, 12


def _cparams(sem):
    return pltpu.CompilerParams(dimension_semantics=sem, vmem_limit_bytes=VMEM_LIMIT)


def _sigmoid(x):
    return 1.0 / (1.0 + jnp.exp(-x))


def _dot(a, b):
    return jnp.dot(a.astype(BF16), b.astype(BF16), preferred_element_type=F32)


def _dot_nt(a, b):
    return lax.dot_general(a.astype(BF16), b.astype(BF16), (((1,), (1,)), ((), ())),
                           preferred_element_type=F32)


def _dot_tn(a, b):
    return lax.dot_general(a.astype(BF16), b.astype(BF16), (((0,), (0,)), ((), ())),
                           preferred_element_type=F32)


def _split3(x):
    hi = x.astype(BF16)
    r1 = x - hi.astype(F32)
    mid = r1.astype(BF16)
    lo = (r1 - mid.astype(F32)).astype(BF16)
    return hi, mid, lo


def _rms_rows(x, g):
    ms = jnp.mean(x * x, axis=-1, keepdims=True)
    return x * lax.rsqrt(ms + EPS) * g


def _inproj_kernel(h_ref, g_ref, w_ref, ws_ref, z_ref, zs_ref, xn_ref, *, tm, tiles_per_seq):
    i = pl.program_id(0)
    j = pl.program_id(1)

    @pl.when(j == 0)
    def _():
        y = _rms_rows(h_ref[...], g_ref[...])
        pos = (i % tiles_per_seq) * tm + lax.broadcasted_iota(jnp.int32, (tm, 1), 0)
        xb = jnp.where(pos >= N_PAD, y, 0.0).astype(BF16)
        xn_ref[...] = xb
        zs_ref[...] = jnp.dot(xb, ws_ref[...], preferred_element_type=F32)

    z_ref[...] = jnp.dot(xn_ref[...], w_ref[...], preferred_element_type=F32).astype(z_ref.dtype)


def _inproj(h, g, w_main, w_small, *, seq_len, tm, tn):
    M, D = h.shape
    N = w_main.shape[1]
    kern = functools.partial(_inproj_kernel, tm=tm, tiles_per_seq=seq_len // tm)
    return pl.pallas_call(
        kern,
        grid=(M // tm, N // tn),
        in_specs=[
            pl.BlockSpec((tm, D), lambda i, j: (i, 0)),
            pl.BlockSpec((1, D), lambda i, j: (0, 0)),
            pl.BlockSpec((D, tn), lambda i, j: (0, j)),
            pl.BlockSpec((D, LANES), lambda i, j: (0, 0)),
        ],
        out_specs=[
            pl.BlockSpec((tm, tn), lambda i, j: (i, j)),
            pl.BlockSpec((tm, LANES), lambda i, j: (i, 0)),
        ],
        out_shape=[jax.ShapeDtypeStruct((M, N), BF16), jax.ShapeDtypeStruct((M, LANES), F32)],
        scratch_shapes=[pltpu.VMEM((tm, D), BF16)],
        compiler_params=_cparams(("parallel", "arbitrary")),
        name="inproj",
    )(h, g, w_main, w_small)


def _gates_kernel(zs_ref, bias_ref, alog_ref, out_ref, carry_ref, *, T):
    t = pl.program_id(1)

    @pl.when(t == 0)
    def _():
        carry_ref[...] = jnp.zeros_like(carry_ref)

    x = zs_ref[0] + bias_ref[...]
    lane = lax.broadcasted_iota(jnp.int32, (T, LANES), 1)
    tail = jnp.log(1.0 + jnp.exp(-jnp.abs(x)))
    log_sig = jnp.minimum(x, 0.0) - tail
    softplus = jnp.maximum(x, 0.0) + tail
    is_f = lane < LANE_GA
    is_g = (lane >= LANE_GA) & (lane < LANE_GB)
    is_b = (lane >= LANE_GB) & (lane < LANE_GB + N_HEADS)
    val_f = jnp.where(is_f, log_sig, 0.0)
    val_g = jnp.where(is_g, -jnp.exp(alog_ref[...]) * softplus, 0.0)

    row = lax.broadcasted_iota(jnp.int32, (T, T), 0)
    col = lax.broadcasted_iota(jnp.int32, (T, T), 1)
    tri_full = (col <= row).astype(BF16)
    tri_chunk = ((col <= row) & ((col // GDN_CHUNK) == (row // GDN_CHUNK))).astype(BF16)

    def cumsum(tri, v):
        hi, mid, lo = _split3(v)
        return (jnp.dot(tri, hi, preferred_element_type=F32)
                + jnp.dot(tri, mid, preferred_element_type=F32)
                + jnp.dot(tri, lo, preferred_element_type=F32))

    c = cumsum(tri_full, val_f) + carry_ref[...]
    carry_ref[...] = c[T - 1:T, :]
    gcum = cumsum(tri_chunk, val_g)
    out_ref[0] = jnp.where(is_f, c, jnp.where(is_g, gcum, jnp.where(is_b, _sigmoid(x), 0.0)))


def _gates(zs, bias_vec, alog_vec, *, T):
    B, L, _ = zs.shape
    return pl.pallas_call(
        functools.partial(_gates_kernel, T=T),
        grid=(B, L // T),
        in_specs=[
            pl.BlockSpec((1, T, LANES), lambda b, t: (b, t, 0)),
            pl.BlockSpec((1, LANES), lambda b, t: (0, 0)),
            pl.BlockSpec((1, LANES), lambda b, t: (0, 0)),
        ],
        out_specs=pl.BlockSpec((1, T, LANES), lambda b, t: (b, t, 0)),
        out_shape=jax.ShapeDtypeStruct((B, L, LANES), F32),
        scratch_shapes=[pltpu.VMEM((1, LANES), F32)],
        compiler_params=_cparams(("parallel", "arbitrary")),
        name="gates",
    )(zs, bias_vec, alog_vec)


LOG2E = 1.4426950408889634
AUG_STRIDE = 8


def _fox_kernel(q_ref, k_ref, v_ref, g_ref, o_ref, kaug, s0, s1, m_s, l_s, acc_s, *, T):
    h = pl.program_id(1)
    qi = pl.program_id(2)
    L = k_ref.shape[1]

    @pl.when(qi == 0)
    def _():
        def build(j, c):
            r0 = pl.multiple_of(j * T, T)
            pos = r0 + lax.broadcasted_iota(jnp.int32, (T, LANES), 0)
            lane = lax.broadcasted_iota(jnp.int32, (T, LANES), 1)
            ck = jnp.where(pos < N_PAD, BIG, g_ref[0, pl.ds(r0, T), :] * LOG2E)
            hi, mid, lo = (x.astype(F32) for x in _split3(ck))
            ext = jnp.where(lane < AUG_STRIDE, hi,
                            jnp.where(lane < 2 * AUG_STRIDE, pltpu.roll(mid, AUG_STRIDE, 1),
                                      jnp.where(lane < 3 * AUG_STRIDE, pltpu.roll(lo, 2 * AUG_STRIDE, 1), 0.0)))
            kaug[pl.ds(r0, T), 0:HEAD_DIM] = k_ref[0, pl.ds(r0, T), :]
            kaug[pl.ds(r0, T), HEAD_DIM:2 * HEAD_DIM] = ext.astype(BF16)
            return c
        lax.fori_loop(0, L // T, build, 0)

    lane = lax.broadcasted_iota(jnp.int32, (T, LANES), 1)
    pick = (lane == h) | (lane == h + AUG_STRIDE) | (lane == h + 2 * AUG_STRIDE)
    q = jnp.concatenate([q_ref[0], jnp.where(pick, -1.0, 0.0).astype(BF16)], axis=1)

    def logits(kb, s_ref):
        start = pl.multiple_of(kb * T, T)
        s_ref[...] = lax.dot_general(q, kaug[pl.ds(start, T), :], (((1,), (1,)), ((), ())),
                                     preferred_element_type=F32)

    def softmax_pv(kb, s_ref, carry, masked=False):
        m, l, acc = carry
        start = pl.multiple_of(kb * T, T)
        s = s_ref[...]
        if masked:
            row = lax.broadcasted_iota(jnp.int32, (T, T), 0)
            col = lax.broadcasted_iota(jnp.int32, (T, T), 1)
            s = jnp.where(col <= row, s, -BIG)
        m_new = jnp.maximum(m, jnp.max(s, axis=1, keepdims=True))
        alpha = jnp.exp2(m - m_new)
        p = jnp.exp2(s - m_new)
        l = alpha * l + jnp.sum(p, axis=1, keepdims=True)
        acc = alpha * acc + jnp.dot(p.astype(BF16), v_ref[0, pl.ds(start, T), :], preferred_element_type=F32)
        return m_new, l, acc

    def finish(carry):
        m, l, acc = carry
        o_ref[0] = (acc / l).astype(o_ref.dtype)

    def pair(j, carry):
        logits(2 * j + 1, s1)
        carry = softmax_pv(2 * j, s0, carry)
        logits(2 * j + 2, s0)
        return softmax_pv(2 * j + 1, s1, carry)

    init = (jnp.full((T, 1), -3.0e38, F32), jnp.zeros((T, 1), F32), jnp.zeros((T, HEAD_DIM), F32))
    logits(0, s0)
    carry = lax.fori_loop(0, qi // 2, pair, init)
    m_s[...], l_s[...], acc_s[...] = carry

    @pl.when(qi % 2 == 0)
    def _():
        finish(softmax_pv(qi, s0, (m_s[...], l_s[...], acc_s[...]), masked=True))

    @pl.when(qi % 2 == 1)
    def _():
        logits(qi, s1)
        carry = softmax_pv(qi - 1, s0, (m_s[...], l_s[...], acc_s[...]))
        finish(softmax_pv(qi, s1, carry, masked=True))


def _fox(zb, gates, *, T):
    B, L, _ = zb.shape
    nblk = L // T
    assert LANE_FF + N_HEADS <= AUG_STRIDE
    return pl.pallas_call(
        functools.partial(_fox_kernel, T=T),
        grid=(B, N_HEADS, nblk),
        in_specs=[
            pl.BlockSpec((1, T, HEAD_DIM), lambda b, h, i: (b, i, U_FQ + h)),
            pl.BlockSpec((1, L, HEAD_DIM), lambda b, h, i: (b, 0, U_FK + h)),
            pl.BlockSpec((1, L, HEAD_DIM), lambda b, h, i: (b, 0, U_FV + h)),
            pl.BlockSpec((1, L, LANES), lambda b, h, i: (b, 0, 0)),
        ],
        out_specs=pl.BlockSpec((1, T, HEAD_DIM), lambda b, h, i: (b, i, h)),
        out_shape=jax.ShapeDtypeStruct((B, L, N_HEADS * HEAD_DIM), BF16),
        scratch_shapes=[pltpu.VMEM((L, 2 * HEAD_DIM), BF16), pltpu.VMEM((T, T), F32), pltpu.VMEM((T, T), F32),
                        pltpu.VMEM((T, 1), F32), pltpu.VMEM((T, 1), F32), pltpu.VMEM((T, HEAD_DIM), F32)],
        compiler_params=_cparams(("parallel", "parallel", "arbitrary")),
        name="fox",
    )(zb, zb, zb, gates)


def _gdn_kernel(q_ref, k_ref, v_ref, qh_ref, kh_ref, vh_ref, wq_ref, wk_ref, wv_ref, gt_ref, gr_ref,
                gate_ref, ng_ref, o_ref, S_ref, buf, *, T, HG):
    hp = pl.program_id(1)
    t = pl.program_id(2)
    C = GDN_CHUNK
    nchunk = T // C

    @pl.when(t == 0)
    def _():
        S_ref[...] = jnp.zeros_like(S_ref)

    def conv_silu(x_ref, halo_ref, w_ref, j):
        lanes = slice(j * HEAD_DIM, (j + 1) * HEAD_DIM)
        halo = jnp.where(t > 0, halo_ref[0, :, lanes].astype(F32), 0.0)
        buf[0:GDN_HALO, :] = halo
        buf[GDN_HALO:GDN_HALO + T, :] = x_ref[0, :, lanes].astype(F32)
        w = w_ref[j]
        y = w[0:1, :] * buf[pl.ds(GDN_HALO - GDN_CONV + 1, T), :]
        for i in range(1, GDN_CONV):
            y = y + w[i:i + 1, :] * buf[pl.ds(GDN_HALO - GDN_CONV + 1 + i, T), :]
        return y * _sigmoid(y)

    def l2n(x):
        return x * lax.rsqrt(jnp.sum(x * x, axis=-1, keepdims=True) + EPS)

    gt = gt_ref[0]
    lane = lax.broadcasted_iota(jnp.int32, (T, LANES), 1)
    q, k, g_b, kb, vb, kg, qd = [], [], [], [], [], [], []
    for j in range(HG):
        h = hp * HG + j
        qj = l2n(conv_silu(q_ref, qh_ref, wq_ref, j)) * (HEAD_DIM ** -0.5)
        kj = l2n(conv_silu(k_ref, kh_ref, wk_ref, j))
        vj = conv_silu(v_ref, vh_ref, wv_ref, j)
        g_col = jnp.sum(jnp.where(lane == LANE_GA + h, gt, 0.0), axis=1, keepdims=True)
        beta = jnp.sum(jnp.where(lane == LANE_GB + h, gt, 0.0), axis=1, keepdims=True)
        gj = jnp.broadcast_to(g_col, (T, HEAD_DIM))
        eg = jnp.exp(gj)
        q.append(qj)
        k.append(kj)
        g_b.append(gj)
        kb.append(kj * beta)
        vb.append(vj * beta)
        kg.append(kj * beta * eg)
        qd.append(qj * eg)

    row = lax.broadcasted_iota(jnp.int32, (C, C), 0)
    col = lax.broadcasted_iota(jnp.int32, (C, C), 1)
    incl = col <= row
    strict = col < row
    eye = (col == row).astype(F32)

    ps = [(j, c) for c in range(nchunk) for j in range(HG)]
    rows = lambda x, jc: x[jc[0]][jc[1] * C:(jc[1] + 1) * C]
    g_last = {jc: g_b[jc[0]][(jc[1] + 1) * C - 1:(jc[1] + 1) * C] for jc in ps}
    decay = {}
    for jc in ps:
        g_row = gr_ref[0, LANE_GA + hp * HG + jc[0], pl.ds(t * nchunk + jc[1], 1), :]
        diff = rows(g_b, jc)[:, 0:C] - g_row
        decay[jc] = jnp.where(incl, jnp.exp(jnp.where(incl, diff, 0.0)), 0.0)
    kq = {jc: _dot_nt(jnp.concatenate([rows(kb, jc), rows(q, jc)], axis=0), rows(k, jc)) for jc in ps}
    npow = {jc: jnp.where(strict, kq[jc][:C] * decay[jc], 0.0) for jc in ps}
    qk = {jc: kq[jc][C:] * decay[jc] for jc in ps}
    p = {jc: eye - npow[jc] for jc in ps}
    for _ in range(5):
        npow = {jc: _dot(npow[jc], npow[jc]) for jc in ps}
        p = {jc: p[jc] + _dot(p[jc], npow[jc]) for jc in ps}
    wu = {jc: _dot(p[jc], jnp.concatenate([rows(kg, jc), rows(vb, jc)], axis=1)) for jc in ps}
    kwu = {jc: _dot_tn(rows(k, jc) * jnp.exp(g_last[jc] - rows(g_b, jc)), wu[jc]) for jc in ps}
    qwu = {jc: _dot(qk[jc], wu[jc]) for jc in ps}
    S = [S_ref[j] for j in range(HG)]
    outs = [[] for _ in range(HG)]
    for jc in ps:
        j = jc[0]
        lhs = jnp.concatenate([kwu[jc][:, :HEAD_DIM], rows(qd, jc) - qwu[jc][:, :HEAD_DIM]], axis=0)
        xs = _dot(lhs, S[j])
        outs[j].append(xs[HEAD_DIM:] + qwu[jc][:, HEAD_DIM:])
        S[j] = S[j] * jnp.exp(g_last[jc]) - xs[:HEAD_DIM] + kwu[jc][:, HEAD_DIM:]

    for j in range(HG):
        S_ref[j] = S[j]
        lanes = slice(j * HEAD_DIM, (j + 1) * HEAD_DIM)
        o = jnp.concatenate(outs[j], axis=0)
        y = o * lax.rsqrt(jnp.mean(o * o, axis=-1, keepdims=True) + EPS) * ng_ref[...]
        gate = gate_ref[0, :, lanes].astype(F32)
        o_ref[0, :, lanes] = (y * (gate * _sigmoid(gate))).astype(o_ref.dtype)


def _gdn(zb, gates, grow, conv_w, norm_g, *, T, HG=2):
    B, L, _ = zb.shape
    nblk = L // T
    hb = T // GDN_HALO
    W = HG * HEAD_DIM

    def cur(u):
        return pl.BlockSpec((1, T, W), lambda b, h, t: (b, t, u // HG + h))

    def halo(u):
        return pl.BlockSpec((1, GDN_HALO, W),
                            lambda b, h, t: (b, jnp.maximum(t * hb - 1, 0), u // HG + h))

    def cw(j):
        return pl.BlockSpec((HG, GDN_CONV, HEAD_DIM), lambda b, h, t: (j * (N_HEADS // HG) + h, 0, 0))

    assert all(u % HG == 0 for u in (U_GQ, U_GK, U_GV, U_GG)) and N_HEADS % HG == 0
    return pl.pallas_call(
        functools.partial(_gdn_kernel, T=T, HG=HG),
        grid=(B, N_HEADS // HG, nblk),
        in_specs=[
            cur(U_GQ), cur(U_GK), cur(U_GV), halo(U_GQ), halo(U_GK), halo(U_GV),
            cw(0), cw(1), cw(2),
            pl.BlockSpec((1, T, LANES), lambda b, h, t: (b, t, 0)),
            pl.BlockSpec((1, grow.shape[1], L // GDN_CHUNK, GDN_CHUNK), lambda b, h, t: (b, 0, 0, 0)),
            cur(U_GG),
            pl.BlockSpec((1, HEAD_DIM), lambda b, h, t: (0, 0)),
        ],
        out_specs=pl.BlockSpec((1, T, W), lambda b, h, t: (b, t, h)),
        out_shape=jax.ShapeDtypeStruct((B, L, N_HEADS * HEAD_DIM), BF16),
        scratch_shapes=[pltpu.VMEM((HG, HEAD_DIM, HEAD_DIM), F32), pltpu.VMEM((T + GDN_HALO, HEAD_DIM), F32)],
        compiler_params=_cparams(("parallel", "parallel", "arbitrary")),
        name="gdn",
    )(zb, zb, zb, zb, zb, zb, conv_w, conv_w, conv_w, gates, grow, zb, norm_g)


def _conf_kernel(a_ref, b_ref, ah_ref, bh_ref, w_ref, wb_ref, lg_ref, lb_ref, o_ref, ubuf, *, T, RC):
    t = pl.program_id(1)
    ah = ah_ref[0].astype(F32)
    bh = bh_ref[0].astype(F32)
    ubuf[0:CONF_HALO, :] = jnp.where(t > 0, ah * _sigmoid(bh), 0.0)
    a = a_ref[0].astype(F32)
    b = b_ref[0].astype(F32)
    ubuf[CONF_HALO:CONF_HALO + T, :] = a * _sigmoid(b)
    base = CONF_HALO - CONF_KERNEL + 1
    for r0 in range(0, T, RC):
        acc = jnp.broadcast_to(wb_ref[...], (RC, CONV_CH))
        for i in range(CONF_KERNEL):
            acc = acc + w_ref[i:i + 1, :] * ubuf[pl.ds(r0 + base + i, RC), :]
        mu = jnp.mean(acc, axis=-1, keepdims=True)
        d = acc - mu
        var = jnp.mean(d * d, axis=-1, keepdims=True)
        un = d * lax.rsqrt(var + EPS) * lg_ref[...] + lb_ref[...]
        o_ref[0, pl.ds(r0, RC), :] = (un * _sigmoid(un)).astype(o_ref.dtype)


def _conf(zb, dw_w, dw_b, ln_g, ln_b, *, T, RC=64):
    B, L, _ = zb.shape
    hb = T // CONF_HALO
    ua, ub = U_CCA * LANES // CONV_CH, U_CCB * LANES // CONV_CH
    vec = pl.BlockSpec((1, CONV_CH), lambda b, t: (0, 0))

    def halo(u):
        return pl.BlockSpec((1, CONF_HALO, CONV_CH), lambda b, t: (b, jnp.maximum(t * hb - 1, 0), u))

    return pl.pallas_call(
        functools.partial(_conf_kernel, T=T, RC=RC),
        grid=(B, L // T),
        in_specs=[
            pl.BlockSpec((1, T, CONV_CH), lambda b, t: (b, t, ua)),
            pl.BlockSpec((1, T, CONV_CH), lambda b, t: (b, t, ub)),
            halo(ua), halo(ub),
            pl.BlockSpec((CONF_KERNEL, CONV_CH), lambda b, t: (0, 0)),
            vec, vec, vec,
        ],
        out_specs=pl.BlockSpec((1, T, CONV_CH), lambda b, t: (b, t, 0)),
        out_shape=jax.ShapeDtypeStruct((B, L, CONV_CH), BF16),
        scratch_shapes=[pltpu.VMEM((T + CONF_HALO, CONV_CH), F32)],
        compiler_params=_cparams(("parallel", "parallel")),
        name="conf",
    )(zb, zb, zb, zb, dw_w, dw_b, ln_g, ln_b)


def _outproj_kernel(fox_ref, gdn_ref, conf_ref, wf_ref, wg_ref, wc_ref, h_ref, o_ref):
    acc = jnp.dot(fox_ref[...], wf_ref[...], preferred_element_type=F32)
    acc = acc + jnp.dot(gdn_ref[...], wg_ref[...], preferred_element_type=F32)
    acc = acc + jnp.dot(conf_ref[...], wc_ref[...], preferred_element_type=F32)
    o_ref[...] = h_ref[...] + acc


def _outproj(fox_o, gdn_o, conf_o, w_out, h, *, tm, tn):
    M, D = h.shape
    W = N_HEADS * HEAD_DIM
    return pl.pallas_call(
        _outproj_kernel,
        grid=(D // tn, M // tm),
        in_specs=[
            pl.BlockSpec((tm, W), lambda j, i: (i, 0)),
            pl.BlockSpec((tm, W), lambda j, i: (i, 0)),
            pl.BlockSpec((tm, CONV_CH), lambda j, i: (i, 0)),
            pl.BlockSpec((W, tn), lambda j, i: (0, j)),
            pl.BlockSpec((W, tn), lambda j, i: (1, j)),
            pl.BlockSpec((CONV_CH, tn), lambda j, i: (2 * W // CONV_CH, j)),
            pl.BlockSpec((tm, tn), lambda j, i: (i, j)),
        ],
        out_specs=pl.BlockSpec((tm, tn), lambda j, i: (i, j)),
        out_shape=jax.ShapeDtypeStruct((M, D), F32),
        compiler_params=_cparams(("parallel", "parallel")),
        name="outproj",
    )(fox_o, gdn_o, conf_o, w_out, w_out, w_out, h)


def _ffn_kernel(h_ref, g_ref, wg_ref, wu_ref, wd_ref, o_ref, xn_ref):
    f = pl.program_id(1)

    @pl.when(f == 0)
    def _():
        x = h_ref[...]
        xn_ref[...] = _rms_rows(x, g_ref[...]).astype(BF16)
        o_ref[...] = x

    xn = xn_ref[...]
    gate = jnp.dot(xn, wg_ref[...], preferred_element_type=F32)
    up = jnp.dot(xn, wu_ref[...], preferred_element_type=F32)
    act = (gate * _sigmoid(gate) * up).astype(BF16)
    o_ref[...] += jnp.dot(act, wd_ref[...], preferred_element_type=F32)


def _ffn(h, g, wg, wu, wd, *, tm, tf):
    M, D = h.shape
    F = wg.shape[1]
    return pl.pallas_call(
        _ffn_kernel,
        grid=(M // tm, F // tf),
        in_specs=[
            pl.BlockSpec((tm, D), lambda i, f: (i, 0)),
            pl.BlockSpec((1, D), lambda i, f: (0, 0)),
            pl.BlockSpec((D, tf), lambda i, f: (0, f)),
            pl.BlockSpec((D, tf), lambda i, f: (0, f)),
            pl.BlockSpec((tf, D), lambda i, f: (f, 0)),
        ],
        out_specs=pl.BlockSpec((tm, D), lambda i, f: (i, 0)),
        out_shape=jax.ShapeDtypeStruct((M, D), F32),
        scratch_shapes=[pltpu.VMEM((tm, D), BF16)],
        compiler_params=_cparams(("parallel", "arbitrary")),
        name="ffn",
    )(h, g, wg, wu, wd)


def _router_kernel(h_ref, g_ref, rhi_ref, rlo_ref, info_ref, cnt_ref, carry_ref, *, tm):
    i = pl.program_id(0)

    @pl.when(i == 0)
    def _():
        carry_ref[...] = jnp.zeros_like(carry_ref)

    xn = _rms_rows(h_ref[...], g_ref[...])
    x_hi = xn.astype(BF16)
    x_lo = (xn - x_hi.astype(F32)).astype(BF16)
    logits = (jnp.dot(x_hi, rhi_ref[...], preferred_element_type=F32)
              + jnp.dot(x_lo, rhi_ref[...], preferred_element_type=F32)
              + jnp.dot(x_hi, rlo_ref[...], preferred_element_type=F32))
    lane = lax.broadcasted_iota(jnp.int32, (tm, LANES), 1).astype(F32)
    logits = jnp.where(lane < N_EXPERTS, logits, -BIG)
    m1 = jnp.max(logits, axis=1, keepdims=True)
    i1 = jnp.min(jnp.where(logits == m1, lane, float(LANES)), axis=1, keepdims=True)
    rest = jnp.where(lane == i1, -BIG, logits)
    m2 = jnp.max(rest, axis=1, keepdims=True)
    i2 = jnp.min(jnp.where(rest == m2, lane, float(LANES)), axis=1, keepdims=True)
    e = jnp.exp(m2 - m1)
    g1 = 1.0 / (1.0 + e)
    g2 = e / (1.0 + e)

    sel = ((lane == i1) | (lane == i2)).astype(F32)
    row = lax.broadcasted_iota(jnp.int32, (tm, tm), 0)
    col = lax.broadcasted_iota(jnp.int32, (tm, tm), 1)
    before = jnp.dot((col < row).astype(BF16), sel.astype(BF16), preferred_element_type=F32)
    ranks = before + carry_ref[...]
    r1 = jnp.sum(jnp.where(lane == i1, ranks, 0.0), axis=1, keepdims=True)
    r2 = jnp.sum(jnp.where(lane == i2, ranks, 0.0), axis=1, keepdims=True)
    carry_ref[...] += jnp.sum(sel, axis=0, keepdims=True)
    cnt_ref[...] = carry_ref[...]

    info = jnp.where(lane == 0, i1, 0.0)
    info = jnp.where(lane == 1, i2, info)
    info = jnp.where(lane == 2, g1, info)
    info = jnp.where(lane == 3, g2, info)
    info = jnp.where(lane == 4, r1, info)
    info = jnp.where(lane == 5, r2, info)
    info_ref[...] = info


def _router(h, g, r_hi, r_lo, *, tm):
    M, D = h.shape
    return pl.pallas_call(
        functools.partial(_router_kernel, tm=tm),
        grid=(M // tm,),
        in_specs=[
            pl.BlockSpec((tm, D), lambda i: (i, 0)),
            pl.BlockSpec((1, D), lambda i: (0, 0)),
            pl.BlockSpec((D, LANES), lambda i: (0, 0)),
            pl.BlockSpec((D, LANES), lambda i: (0, 0)),
        ],
        out_specs=[
            pl.BlockSpec((tm, LANES), lambda i: (i, 0)),
            pl.BlockSpec((1, LANES), lambda i: (0, 0)),
        ],
        out_shape=[jax.ShapeDtypeStruct((M, LANES), F32), jax.ShapeDtypeStruct((1, LANES), F32)],
        scratch_shapes=[pltpu.VMEM((1, LANES), F32)],
        compiler_params=_cparams(("arbitrary",)),
        name="router",
    )(h, g, r_hi, r_lo)


PACK_SHIFT = 15
PACK = 1 << PACK_SHIFT
SCATTER_GROUPS = 8


def _moe_kernel(te_ref, nv_ref, pk_ref, h_hbm, g_ref, wg_ref, wu_ref, wd_ref, y_hbm,
                xg, xn, acc, sem_g, sem_s, *, tm, nf):
    i = pl.program_id(0)
    f = pl.program_id(1)
    nv = nv_ref[0]
    valid = i < nv

    def start_gather(tile):
        def body(r, c):
            tok = pk_ref[tile * tm + r] & (PACK - 1)
            pltpu.make_async_copy(h_hbm.at[pl.ds(tok, 1), :], xg.at[pl.ds(r, 1), :], sem_g).start()
            return c
        lax.fori_loop(0, tm, body, 0, unroll=8)

    def wait_gather():
        pltpu.make_async_copy(h_hbm.at[pl.ds(0, tm), :], xg, sem_g).wait()

    def wait_scatter():
        pltpu.make_async_copy(acc, y_hbm.at[pl.ds(0, tm), :], sem_s).wait()

    @pl.when((i == 0) & (f == 0))
    def _():
        acc[...] = jnp.zeros_like(acc)
        spare = pltpu.make_async_copy(acc, y_hbm.at[pl.ds(y_hbm.shape[0] - tm, tm), :], sem_s)
        spare.start()
        spare.wait()

    @pl.when(valid & (f == 0))
    def _():
        @pl.when(i == 0)
        def _():
            start_gather(i)

        wait_gather()
        xn[...] = _rms_rows(xg[...], g_ref[...]).astype(BF16)

    n_issue = max(d for d in (1, 2, 4, 8) if d <= nf - 2 and tm % d == 0)
    chunk = tm // n_issue
    prefetch = (i + 1 < nv) & (f >= 1) & (f <= n_issue)
    last = f == nf - 1

    def issue_chunk():
        row0 = (f - 1) * chunk
        for r in range(chunk):
            tok = pk_ref[(i + 1) * tm + row0 + r] & (PACK - 1)
            pltpu.make_async_copy(h_hbm.at[pl.ds(tok, 1), :], xg.at[pl.ds(row0 + r, 1), :], sem_g).start()

    def hidden():
        x = xn[...]
        gate = jnp.dot(x, wg_ref[0].astype(BF16), preferred_element_type=F32)
        up = jnp.dot(x, wu_ref[0].astype(BF16), preferred_element_type=F32)
        return (gate * _sigmoid(gate) * up).astype(BF16)

    @pl.when(valid & (f == 0))
    def _():
        act = hidden()

        @pl.when(i > 0)
        def _():
            wait_scatter()

        acc[...] = jnp.dot(act, wd_ref[0].astype(BF16), preferred_element_type=F32)

    @pl.when(valid & (f > 0) & prefetch)
    def _():
        issue_chunk()
        acc[...] += jnp.dot(hidden(), wd_ref[0].astype(BF16), preferred_element_type=F32)

    @pl.when(valid & (f > 0) & jnp.logical_not(prefetch) & jnp.logical_not(last))
    def _():
        acc[...] += jnp.dot(hidden(), wd_ref[0].astype(BF16), preferred_element_type=F32)

    @pl.when(valid & last)
    def _():
        act = hidden()
        wd = wd_ref[0].astype(BF16)
        group = tm // SCATTER_GROUPS
        for c in range(SCATTER_GROUPS):
            rows = slice(c * group, (c + 1) * group)
            acc[rows, :] += jnp.dot(act[rows], wd, preferred_element_type=F32)
            for r in range(c * group, (c + 1) * group):
                dst = pk_ref[i * tm + r] >> PACK_SHIFT
                pltpu.make_async_copy(acc.at[pl.ds(r, 1), :], y_hbm.at[pl.ds(dst, 1), :], sem_s).start()

        @pl.when(i == nv - 1)
        def _():
            wait_scatter()


def _moe(tile_expert, n_valid, packed, h, g, wg, wu, wd, *, tm, tf, n_tiles):
    M, D = h.shape
    F = wg.shape[2]
    nf = F // tf
    assert nf >= 3 and tm % SCATTER_GROUPS == 0

    def fblk(i, f, nv):
        return jnp.where(i < nv[0], f, nf - 1)

    return pl.pallas_call(
        functools.partial(_moe_kernel, tm=tm, nf=nf),
        grid_spec=pltpu.PrefetchScalarGridSpec(
            num_scalar_prefetch=3,
            grid=(n_tiles, nf),
            in_specs=[
                pl.BlockSpec(memory_space=pl.ANY),
                pl.BlockSpec((1, D), lambda i, f, te, nv, pk: (0, 0)),
                pl.BlockSpec((1, D, tf), lambda i, f, te, nv, pk: (te[i], 0, fblk(i, f, nv))),
                pl.BlockSpec((1, D, tf), lambda i, f, te, nv, pk: (te[i], 0, fblk(i, f, nv))),
                pl.BlockSpec((1, tf, D), lambda i, f, te, nv, pk: (te[i], fblk(i, f, nv), 0)),
            ],
            out_specs=pl.BlockSpec(memory_space=pl.ANY),
            scratch_shapes=[
                pltpu.VMEM((tm, D), F32), pltpu.VMEM((tm, D), BF16), pltpu.VMEM((tm, D), F32),
                pltpu.SemaphoreType.DMA, pltpu.SemaphoreType.DMA,
            ],
        ),
        out_shape=jax.ShapeDtypeStruct((2 * M + tm, D), F32),
        compiler_params=_cparams(("arbitrary", "arbitrary")),
        name="moe",
    )(tile_expert, n_valid, packed, h, g, wg, wu, wd)


def _combine_kernel(h_ref, y0_ref, y1_ref, info_ref, gf_ref, o_ref):
    info = info_ref[...]
    hn = h_ref[...] + info[:, 2:3] * y0_ref[...] + info[:, 3:4] * y1_ref[...]
    o_ref[...] = _rms_rows(hn, gf_ref[...])


def _combine(h, y, info, gf, *, batch, seq_len):
    M, D = h.shape
    tb = FRONT
    per_seq = seq_len // tb
    real = per_seq - 1

    def src(off):
        return lambda b, j: (off + b * per_seq + 1 + j, 0)

    return pl.pallas_call(
        _combine_kernel,
        grid=(batch, real),
        in_specs=[
            pl.BlockSpec((tb, D), src(0)),
            pl.BlockSpec((tb, D), src(0)),
            pl.BlockSpec((tb, D), src(M // tb)),
            pl.BlockSpec((tb, LANES), src(0)),
            pl.BlockSpec((1, D), lambda b, j: (0, 0)),
        ],
        out_specs=pl.BlockSpec((tb, D), lambda b, j: (b * real + j, 0)),
        out_shape=jax.ShapeDtypeStruct((batch * real * tb, D), F32),
        compiler_params=_cparams(("parallel", "parallel")),
        name="combine",
    )(h, y, y, info, gf)


def _prep_w_in(w):
    W = N_HEADS * HEAD_DIM
    sizes = [W, W, W, N_HEADS, W, W, W, N_HEADS, N_HEADS, W, 2 * CONV_CH]
    parts, o = [], 0
    for s in sizes:
        parts.append(w[:, o:o + s])
        o += s
    fq, fk, fv, ff, gq, gk, gv, ga, gb, gg, cc = parts
    main = jnp.concatenate([cc, fq * (HEAD_DIM ** -0.5 * LOG2E), fk, fv, gq, gk, gv, gg], axis=1).astype(BF16)
    small = jnp.concatenate([ff, ga, gb], axis=1)
    small = jnp.pad(small, ((0, 0), (0, LANES - small.shape[1]))).astype(BF16)
    return main, small


def _lane_vec(pairs):
    v = jnp.zeros((1, LANES), F32)
    for lane0, vals in pairs:
        v = lax.dynamic_update_slice(v, vals.astype(F32)[None, :], (0, lane0))
    return v


def _mixer(h, layer, p, *, B, L, T, tm):
    zb, zs = _inproj(h, p["attn_norm_g"][layer][None, :], *p["w_in"][layer], seq_len=L, tm=tm, tn=1280)
    zb = zb.reshape(B, L, N_MAIN)
    bias_vec = _lane_vec([(LANE_FF, p["fox_b_f"][layer]), (LANE_GA, p["gdn_dt_bias"][layer])])
    alog_vec = _lane_vec([(LANE_GA, p["gdn_A_log"][layer])])
    gates = _gates(zs.reshape(B, L, LANES), bias_vec, alog_vec, T=T)
    gates_t = jnp.swapaxes(gates[:, :, :24], 1, 2)
    grow = gates_t.reshape(B, 24, L // GDN_CHUNK, GDN_CHUNK)
    fox_o = _fox(zb, gates, T=T)
    conv_w = p["gdn_conv_w"][layer].reshape(GDN_CONV, 3 * N_HEADS, HEAD_DIM).transpose(1, 0, 2)
    gdn_o = _gdn(zb, gates, grow, conv_w, p["gdn_norm_g"][layer][None, :], T=T)
    conf_o = _conf(zb, p["cc_dw_w"][layer], p["cc_dw_b"][layer][None, :], p["cc_ln_g"][layer][None, :],
                   p["cc_ln_b"][layer][None, :], T=T)
    M = B * L
    return _outproj(fox_o.reshape(M, -1), gdn_o.reshape(M, -1), conf_o.reshape(M, -1),
                    p["w_out"][layer], h, tm=tm, tn=1024)


def _moe_layer(h, i, p, *, B, L, tm_r, tm_e, tf_e):
    M, D = h.shape
    g = p["ffn_norm_g_moe"][i][None, :]
    router = jnp.pad(p["moe_router"][i], ((0, 0), (0, LANES - N_EXPERTS)))
    r_hi = router.astype(BF16)
    r_lo = (router - r_hi.astype(F32)).astype(BF16)
    info, cnt = _router(h, g, r_hi, r_lo, tm=tm_r)

    e = info[:, 0:2].astype(jnp.int32)
    rank = info[:, 4:6].astype(jnp.int32)
    counts = cnt[0, :N_EXPERTS].astype(jnp.int32)
    padded = ((counts + tm_e - 1) // tm_e) * tm_e
    ends = jnp.cumsum(padded)
    offs = ends - padded
    pos = offs[e] + rank
    n_tiles = (2 * M + N_EXPERTS * (tm_e - 1)) // tm_e
    P = n_tiles * tm_e
    tok = jnp.arange(M, dtype=jnp.int32)[:, None]
    dst = jnp.arange(2, dtype=jnp.int32)[None, :] * M + tok
    word = dst * PACK + tok
    filler = (2 * M + jnp.arange(P, dtype=jnp.int32) % tm_e) * PACK
    packed = filler.at[pos.reshape(-1)].set(word.reshape(-1))
    n_valid = (ends[-1] // tm_e).astype(jnp.int32)
    tile_start = jnp.minimum(jnp.arange(n_tiles, dtype=jnp.int32), n_valid - 1) * tm_e
    tile_expert = jnp.minimum(jnp.searchsorted(ends, tile_start, side="right"),
                              N_EXPERTS - 1).astype(jnp.int32)

    y = _moe(tile_expert, n_valid[None], packed, h, g, p["moe_w_gate"][i], p["moe_w_up"][i],
             p["moe_w_down"][i], tm=tm_e, tf=tf_e, n_tiles=n_tiles)
    return _combine(h, y, info, p["final_norm_g"][None, :], batch=B, seq_len=L)


def kernel(x, meta_tokens, attn_norm_g, w_in, fox_b_f, gdn_conv_w, gdn_A_log, gdn_dt_bias, gdn_norm_g,
           cc_dw_w, cc_dw_b, cc_ln_g, cc_ln_b, w_out, ffn_norm_g, dense_w_gate, dense_w_up,
           dense_w_down, moe_router, moe_w_gate, moe_w_up, moe_w_down, final_norm_g):
    B, S, D = x.shape
    depth = w_in.shape[0]
    assert depth == 2, "layer 0 is dense SwiGLU, layer 1 is the routed expert SwiGLU followed by the final norm"
    L = S + FRONT
    T = 640 if L % 640 == 0 else 128
    tm = T
    p = dict(
        attn_norm_g=attn_norm_g, fox_b_f=fox_b_f, gdn_conv_w=gdn_conv_w, gdn_A_log=gdn_A_log,
        gdn_dt_bias=gdn_dt_bias, gdn_norm_g=gdn_norm_g, cc_dw_w=cc_dw_w, cc_dw_b=cc_dw_b,
        cc_ln_g=cc_ln_g, cc_ln_b=cc_ln_b, w_out=w_out.astype(BF16),
        w_in=[_prep_w_in(w_in[l]) for l in range(depth)],
        ffn_norm_g_moe=ffn_norm_g[1::2], moe_router=moe_router,
        moe_w_gate=moe_w_gate, moe_w_up=moe_w_up, moe_w_down=moe_w_down, final_norm_g=final_norm_g,
    )
    pad = jnp.zeros((B, N_PAD, D), x.dtype)
    meta = jnp.broadcast_to(meta_tokens.astype(x.dtype)[None], (B, N_META, D))
    h = jnp.concatenate([pad, meta, x], axis=1).reshape(B * L, D)

    h = _mixer(h, 0, p, B=B, L=L, T=T, tm=tm)
    h = _ffn(h, ffn_norm_g[0][None, :], dense_w_gate[0].astype(BF16), dense_w_up[0].astype(BF16),
             dense_w_down[0].astype(BF16), tm=tm, tf=512)
    h = _mixer(h, 1, p, B=B, L=L, T=T, tm=tm)
    out = _moe_layer(h, 0, p, B=B, L=L, tm_r=tm, tm_e=1024, tf_e=512)
    return out.reshape(B, S, D)
```

```python
import functools

import jax
import jax.numpy as jnp
from jax import lax
from jax.experimental import pallas as pl
from jax.experimental.pallas import tpu as pltpu

F32 = jnp.float32
BF16 = jnp.bfloat16

HEAD_DIM = 128
N_HEADS = 6
CONV_CH = 512
GDN_CONV = 4
CONF_KERNEL = 31
GDN_CHUNK = 64
N_META = 16
FRONT = 128
N_PAD = FRONT - N_META
N_EXPERTS = 8
EPS = 1e-6
BIG = 1e30
LANES = 128
CONF_HALO = 32
GDN_HALO = 16
VMEM_LIMIT = 56 * 1024 * 1024

U_CCA, U_CCB, U_FQ, U_FK, U_FV, U_GQ, U_GK, U_GV, U_GG = 0, 4, 8, 14, 20, 26, 32, 38, 44
N_MAIN = 50 * LANES
LANE_FF, LANE_GA, LANE_GB = 0, 6, 12


def _cparams(sem):
    return pltpu.CompilerParams(dimension_semantics=sem, vmem_limit_bytes=VMEM_LIMIT)


def _sigmoid(x):
    return 1.0 / (1.0 + jnp.exp(-x))


def _dot(a, b):
    return jnp.dot(a.astype(BF16), b.astype(BF16), preferred_element_type=F32)


def _dot_nt(a, b):
    return lax.dot_general(a.astype(BF16), b.astype(BF16), (((1,), (1,)), ((), ())),
                           preferred_element_type=F32)


def _dot_tn(a, b):
    return lax.dot_general(a.astype(BF16), b.astype(BF16), (((0,), (0,)), ((), ())),
                           preferred_element_type=F32)


def _split3(x):
    hi = x.astype(BF16)
    r1 = x - hi.astype(F32)
    mid = r1.astype(BF16)
    lo = (r1 - mid.astype(F32)).astype(BF16)
    return hi, mid, lo


def _rms_rows(x, g):
    ms = jnp.mean(x * x, axis=-1, keepdims=True)
    return x * lax.rsqrt(ms + EPS) * g


def _inproj_kernel(h_ref, g_ref, w_ref, ws_ref, z_ref, zs_ref, xn_ref, *, tm, tiles_per_seq):
    i = pl.program_id(0)
    j = pl.program_id(1)

    @pl.when(j == 0)
    def _():
        y = _rms_rows(h_ref[...], g_ref[...])
        pos = (i % tiles_per_seq) * tm + lax.broadcasted_iota(jnp.int32, (tm, 1), 0)
        xb = jnp.where(pos >= N_PAD, y, 0.0).astype(BF16)
        xn_ref[...] = xb
        zs_ref[...] = jnp.dot(xb, ws_ref[...], preferred_element_type=F32)

    z_ref[...] = jnp.dot(xn_ref[...], w_ref[...], preferred_element_type=F32).astype(z_ref.dtype)


def _inproj(h, g, w_main, w_small, *, seq_len, tm, tn):
    M, D = h.shape
    N = w_main.shape[1]
    kern = functools.partial(_inproj_kernel, tm=tm, tiles_per_seq=seq_len // tm)
    return pl.pallas_call(
        kern,
        grid=(M // tm, N // tn),
        in_specs=[
            pl.BlockSpec((tm, D), lambda i, j: (i, 0)),
            pl.BlockSpec((1, D), lambda i, j: (0, 0)),
            pl.BlockSpec((D, tn), lambda i, j: (0, j)),
            pl.BlockSpec((D, LANES), lambda i, j: (0, 0)),
        ],
        out_specs=[
            pl.BlockSpec((tm, tn), lambda i, j: (i, j)),
            pl.BlockSpec((tm, LANES), lambda i, j: (i, 0)),
        ],
        out_shape=[jax.ShapeDtypeStruct((M, N), BF16), jax.ShapeDtypeStruct((M, LANES), F32)],
        scratch_shapes=[pltpu.VMEM((tm, D), BF16)],
        compiler_params=_cparams(("parallel", "arbitrary")),
        name="inproj",
    )(h, g, w_main, w_small)


def _gates_kernel(zs_ref, bias_ref, alog_ref, out_ref, carry_ref, *, T):
    t = pl.program_id(1)

    @pl.when(t == 0)
    def _():
        carry_ref[...] = jnp.zeros_like(carry_ref)

    x = zs_ref[0] + bias_ref[...]
    lane = lax.broadcasted_iota(jnp.int32, (T, LANES), 1)
    tail = jnp.log(1.0 + jnp.exp(-jnp.abs(x)))
    log_sig = jnp.minimum(x, 0.0) - tail
    softplus = jnp.maximum(x, 0.0) + tail
    is_f = lane < LANE_GA
    is_g = (lane >= LANE_GA) & (lane < LANE_GB)
    is_b = (lane >= LANE_GB) & (lane < LANE_GB + N_HEADS)
    val_f = jnp.where(is_f, log_sig, 0.0)
    val_g = jnp.where(is_g, -jnp.exp(alog_ref[...]) * softplus, 0.0)

    row = lax.broadcasted_iota(jnp.int32, (T, T), 0)
    col = lax.broadcasted_iota(jnp.int32, (T, T), 1)
    tri_full = (col <= row).astype(BF16)
    tri_chunk = ((col <= row) & ((col // GDN_CHUNK) == (row // GDN_CHUNK))).astype(BF16)

    def cumsum(tri, v):
        hi, mid, lo = _split3(v)
        return (jnp.dot(tri, hi, preferred_element_type=F32)
                + jnp.dot(tri, mid, preferred_element_type=F32)
                + jnp.dot(tri, lo, preferred_element_type=F32))

    c = cumsum(tri_full, val_f) + carry_ref[...]
    carry_ref[...] = c[T - 1:T, :]
    gcum = cumsum(tri_chunk, val_g)
    out_ref[0] = jnp.where(is_f, c, jnp.where(is_g, gcum, jnp.where(is_b, _sigmoid(x), 0.0)))


def _gates(zs, bias_vec, alog_vec, *, T):
    B, L, _ = zs.shape
    return pl.pallas_call(
        functools.partial(_gates_kernel, T=T),
        grid=(B, L // T),
        in_specs=[
            pl.BlockSpec((1, T, LANES), lambda b, t: (b, t, 0)),
            pl.BlockSpec((1, LANES), lambda b, t: (0, 0)),
            pl.BlockSpec((1, LANES), lambda b, t: (0, 0)),
        ],
        out_specs=pl.BlockSpec((1, T, LANES), lambda b, t: (b, t, 0)),
        out_shape=jax.ShapeDtypeStruct((B, L, LANES), F32),
        scratch_shapes=[pltpu.VMEM((1, LANES), F32)],
        compiler_params=_cparams(("parallel", "arbitrary")),
        name="gates",
    )(zs, bias_vec, alog_vec)


LOG2E = 1.4426950408889634
AUG_STRIDE = 8


def _fox_kernel(q_ref, k_ref, v_ref, g_ref, o_ref, kaug, s0, s1, m_s, l_s, acc_s, *, T):
    h = pl.program_id(1)
    qi = pl.program_id(2)
    L = k_ref.shape[1]

    @pl.when(qi == 0)
    def _():
        def build(j, c):
            r0 = pl.multiple_of(j * T, T)
            pos = r0 + lax.broadcasted_iota(jnp.int32, (T, LANES), 0)
            lane = lax.broadcasted_iota(jnp.int32, (T, LANES), 1)
            ck = jnp.where(pos < N_PAD, BIG, g_ref[0, pl.ds(r0, T), :] * LOG2E)
            hi, mid, lo = (x.astype(F32) for x in _split3(ck))
            ext = jnp.where(lane < AUG_STRIDE, hi,
                            jnp.where(lane < 2 * AUG_STRIDE, pltpu.roll(mid, AUG_STRIDE, 1),
                                      jnp.where(lane < 3 * AUG_STRIDE, pltpu.roll(lo, 2 * AUG_STRIDE, 1), 0.0)))
            kaug[pl.ds(r0, T), 0:HEAD_DIM] = k_ref[0, pl.ds(r0, T), :]
            kaug[pl.ds(r0, T), HEAD_DIM:2 * HEAD_DIM] = ext.astype(BF16)
            return c
        lax.fori_loop(0, L // T, build, 0)

    lane = lax.broadcasted_iota(jnp.int32, (T, LANES), 1)
    pick = (lane == h) | (lane == h + AUG_STRIDE) | (lane == h + 2 * AUG_STRIDE)
    q = jnp.concatenate([q_ref[0], jnp.where(pick, -1.0, 0.0).astype(BF16)], axis=1)

    def logits(kb, s_ref):
        start = pl.multiple_of(kb * T, T)
        s_ref[...] = lax.dot_general(q, kaug[pl.ds(start, T), :], (((1,), (1,)), ((), ())),
                                     preferred_element_type=F32)

    def softmax_pv(kb, s_ref, carry, masked=False):
        m, l, acc = carry
        start = pl.multiple_of(kb * T, T)
        s = s_ref[...]
        if masked:
            row = lax.broadcasted_iota(jnp.int32, (T, T), 0)
            col = lax.broadcasted_iota(jnp.int32, (T, T), 1)
            s = jnp.where(col <= row, s, -BIG)
        m_new = jnp.maximum(m, jnp.max(s, axis=1, keepdims=True))
        alpha = jnp.exp2(m - m_new)
        p = jnp.exp2(s - m_new)
        l = alpha * l + jnp.sum(p, axis=1, keepdims=True)
        acc = alpha * acc + jnp.dot(p.astype(BF16), v_ref[0, pl.ds(start, T), :], preferred_element_type=F32)
        return m_new, l, acc

    def finish(carry):
        m, l, acc = carry
        o_ref[0] = (acc / l).astype(o_ref.dtype)

    def pair(j, carry):
        logits(2 * j + 1, s1)
        carry = softmax_pv(2 * j, s0, carry)
        logits(2 * j + 2, s0)
        return softmax_pv(2 * j + 1, s1, carry)

    init = (jnp.full((T, 1), -3.0e38, F32), jnp.zeros((T, 1), F32), jnp.zeros((T, HEAD_DIM), F32))
    logits(0, s0)
    carry = lax.fori_loop(0, qi // 2, pair, init)
    m_s[...], l_s[...], acc_s[...] = carry

    @pl.when(qi % 2 == 0)
    def _():
        finish(softmax_pv(qi, s0, (m_s[...], l_s[...], acc_s[...]), masked=True))

    @pl.when(qi % 2 == 1)
    def _():
        logits(qi, s1)
        carry = softmax_pv(qi - 1, s0, (m_s[...], l_s[...], acc_s[...]))
        finish(softmax_pv(qi, s1, carry, masked=True))


def _fox(zb, gates, *, T):
    B, L, _ = zb.shape
    nblk = L // T
    assert LANE_FF + N_HEADS <= AUG_STRIDE
    return pl.pallas_call(
        functools.partial(_fox_kernel, T=T),
        grid=(B, N_HEADS, nblk),
        in_specs=[
            pl.BlockSpec((1, T, HEAD_DIM), lambda b, h, i: (b, i, U_FQ + h)),
            pl.BlockSpec((1, L, HEAD_DIM), lambda b, h, i: (b, 0, U_FK + h)),
            pl.BlockSpec((1, L, HEAD_DIM), lambda b, h, i: (b, 0, U_FV + h)),
            pl.BlockSpec((1, L, LANES), lambda b, h, i: (b, 0, 0)),
        ],
        out_specs=pl.BlockSpec((1, T, HEAD_DIM), lambda b, h, i: (b, i, h)),
        out_shape=jax.ShapeDtypeStruct((B, L, N_HEADS * HEAD_DIM), BF16),
        scratch_shapes=[pltpu.VMEM((L, 2 * HEAD_DIM), BF16), pltpu.VMEM((T, T), F32), pltpu.VMEM((T, T), F32),
                        pltpu.VMEM((T, 1), F32), pltpu.VMEM((T, 1), F32), pltpu.VMEM((T, HEAD_DIM), F32)],
        compiler_params=_cparams(("parallel", "parallel", "arbitrary")),
        name="fox",
    )(zb, zb, zb, gates)


def _gdn_kernel(q_ref, k_ref, v_ref, qh_ref, kh_ref, vh_ref, wq_ref, wk_ref, wv_ref, gt_ref, gr_ref,
                gate_ref, ng_ref, o_ref, S_ref, buf, *, T, HG):
    hp = pl.program_id(1)
    t = pl.program_id(2)
    C = GDN_CHUNK
    nchunk = T // C

    @pl.when(t == 0)
    def _():
        S_ref[...] = jnp.zeros_like(S_ref)

    def conv_silu(x_ref, halo_ref, w_ref, j):
        lanes = slice(j * HEAD_DIM, (j + 1) * HEAD_DIM)
        halo = jnp.where(t > 0, halo_ref[0, :, lanes].astype(F32), 0.0)
        buf[0:GDN_HALO, :] = halo
        buf[GDN_HALO:GDN_HALO + T, :] = x_ref[0, :, lanes].astype(F32)
        w = w_ref[j]
        y = w[0:1, :] * buf[pl.ds(GDN_HALO - GDN_CONV + 1, T), :]
        for i in range(1, GDN_CONV):
            y = y + w[i:i + 1, :] * buf[pl.ds(GDN_HALO - GDN_CONV + 1 + i, T), :]
        return y * _sigmoid(y)

    def l2n(x):
        return x * lax.rsqrt(jnp.sum(x * x, axis=-1, keepdims=True) + EPS)

    gt = gt_ref[0]
    lane = lax.broadcasted_iota(jnp.int32, (T, LANES), 1)
    q, k, g_b, kb, vb, kg, qd = [], [], [], [], [], [], []
    for j in range(HG):
        h = hp * HG + j
        qj = l2n(conv_silu(q_ref, qh_ref, wq_ref, j)) * (HEAD_DIM ** -0.5)
        kj = l2n(conv_silu(k_ref, kh_ref, wk_ref, j))
        vj = conv_silu(v_ref, vh_ref, wv_ref, j)
        g_col = jnp.sum(jnp.where(lane == LANE_GA + h, gt, 0.0), axis=1, keepdims=True)
        beta = jnp.sum(jnp.where(lane == LANE_GB + h, gt, 0.0), axis=1, keepdims=True)
        gj = jnp.broadcast_to(g_col, (T, HEAD_DIM))
        eg = jnp.exp(gj)
        q.append(qj)
        k.append(kj)
        g_b.append(gj)
        kb.append(kj * beta)
        vb.append(vj * beta)
        kg.append(kj * beta * eg)
        qd.append(qj * eg)

    row = lax.broadcasted_iota(jnp.int32, (C, C), 0)
    col = lax.broadcasted_iota(jnp.int32, (C, C), 1)
    incl = col <= row
    strict = col < row
    eye = (col == row).astype(F32)

    ps = [(j, c) for c in range(nchunk) for j in range(HG)]
    rows = lambda x, jc: x[jc[0]][jc[1] * C:(jc[1] + 1) * C]
    g_last = {jc: g_b[jc[0]][(jc[1] + 1) * C - 1:(jc[1] + 1) * C] for jc in ps}
    decay = {}
    for jc in ps:
        g_row = gr_ref[0, LANE_GA + hp * HG + jc[0], pl.ds(t * nchunk + jc[1], 1), :]
        diff = rows(g_b, jc)[:, 0:C] - g_row
        decay[jc] = jnp.where(incl, jnp.exp(jnp.where(incl, diff, 0.0)), 0.0)
    kq = {jc: _dot_nt(jnp.concatenate([rows(kb, jc), rows(q, jc)], axis=0), rows(k, jc)) for jc in ps}
    npow = {jc: jnp.where(strict, kq[jc][:C] * decay[jc], 0.0) for jc in ps}
    qk = {jc: kq[jc][C:] * decay[jc] for jc in ps}
    p = {jc: eye - npow[jc] for jc in ps}
    for _ in range(5):
        npow = {jc: _dot(npow[jc], npow[jc]) for jc in ps}
        p = {jc: p[jc] + _dot(p[jc], npow[jc]) for jc in ps}
    wu = {jc: _dot(p[jc], jnp.concatenate([rows(kg, jc), rows(vb, jc)], axis=1)) for jc in ps}
    kwu = {jc: _dot_tn(rows(k, jc) * jnp.exp(g_last[jc] - rows(g_b, jc)), wu[jc]) for jc in ps}
    qwu = {jc: _dot(qk[jc], wu[jc]) for jc in ps}
    S = [S_ref[j] for j in range(HG)]
    outs = [[] for _ in range(HG)]
    for jc in ps:
        j = jc[0]
        lhs = jnp.concatenate([kwu[jc][:, :HEAD_DIM], rows(qd, jc) - qwu[jc][:, :HEAD_DIM]], axis=0)
        xs = _dot(lhs, S[j])
        outs[j].append(xs[HEAD_DIM:] + qwu[jc][:, HEAD_DIM:])
        S[j] = S[j] * jnp.exp(g_last[jc]) - xs[:HEAD_DIM] + kwu[jc][:, HEAD_DIM:]

    for j in range(HG):
        S_ref[j] = S[j]
        lanes = slice(j * HEAD_DIM, (j + 1) * HEAD_DIM)
        o = jnp.concatenate(outs[j], axis=0)
        y = o * lax.rsqrt(jnp.mean(o * o, axis=-1, keepdims=True) + EPS) * ng_ref[...]
        gate = gate_ref[0, :, lanes].astype(F32)
        o_ref[0, :, lanes] = (y * (gate * _sigmoid(gate))).astype(o_ref.dtype)


def _gdn(zb, gates, grow, conv_w, norm_g, *, T, HG=2):
    B, L, _ = zb.shape
    nblk = L // T
    hb = T // GDN_HALO
    W = HG * HEAD_DIM

    def cur(u):
        return pl.BlockSpec((1, T, W), lambda b, h, t: (b, t, u // HG + h))

    def halo(u):
        return pl.BlockSpec((1, GDN_HALO, W),
                            lambda b, h, t: (b, jnp.maximum(t * hb - 1, 0), u // HG + h))

    def cw(j):
        return pl.BlockSpec((HG, GDN_CONV, HEAD_DIM), lambda b, h, t: (j * (N_HEADS // HG) + h, 0, 0))

    assert all(u % HG == 0 for u in (U_GQ, U_GK, U_GV, U_GG)) and N_HEADS % HG == 0
    return pl.pallas_call(
        functools.partial(_gdn_kernel, T=T, HG=HG),
        grid=(B, N_HEADS // HG, nblk),
        in_specs=[
            cur(U_GQ), cur(U_GK), cur(U_GV), halo(U_GQ), halo(U_GK), halo(U_GV),
            cw(0), cw(1), cw(2),
            pl.BlockSpec((1, T, LANES), lambda b, h, t: (b, t, 0)),
            pl.BlockSpec((1, grow.shape[1], L // GDN_CHUNK, GDN_CHUNK), lambda b, h, t: (b, 0, 0, 0)),
            cur(U_GG),
            pl.BlockSpec((1, HEAD_DIM), lambda b, h, t: (0, 0)),
        ],
        out_specs=pl.BlockSpec((1, T, W), lambda b, h, t: (b, t, h)),
        out_shape=jax.ShapeDtypeStruct((B, L, N_HEADS * HEAD_DIM), BF16),
        scratch_shapes=[pltpu.VMEM((HG, HEAD_DIM, HEAD_DIM), F32), pltpu.VMEM((T + GDN_HALO, HEAD_DIM), F32)],
        compiler_params=_cparams(("parallel", "parallel", "arbitrary")),
        name="gdn",
    )(zb, zb, zb, zb, zb, zb, conv_w, conv_w, conv_w, gates, grow, zb, norm_g)


def _conf_kernel(a_ref, b_ref, ah_ref, bh_ref, w_ref, wb_ref, lg_ref, lb_ref, o_ref, ubuf, *, T, RC):
    t = pl.program_id(1)
    ah = ah_ref[0].astype(F32)
    bh = bh_ref[0].astype(F32)
    ubuf[0:CONF_HALO, :] = jnp.where(t > 0, ah * _sigmoid(bh), 0.0)
    a = a_ref[0].astype(F32)
    b = b_ref[0].astype(F32)
    ubuf[CONF_HALO:CONF_HALO + T, :] = a * _sigmoid(b)
    base = CONF_HALO - CONF_KERNEL + 1
    for r0 in range(0, T, RC):
        acc = jnp.broadcast_to(wb_ref[...], (RC, CONV_CH))
        for i in range(CONF_KERNEL):
            acc = acc + w_ref[i:i + 1, :] * ubuf[pl.ds(r0 + base + i, RC), :]
        mu = jnp.mean(acc, axis=-1, keepdims=True)
        d = acc - mu
        var = jnp.mean(d * d, axis=-1, keepdims=True)
        un = d * lax.rsqrt(var + EPS) * lg_ref[...] + lb_ref[...]
        o_ref[0, pl.ds(r0, RC), :] = (un * _sigmoid(un)).astype(o_ref.dtype)


def _conf(zb, dw_w, dw_b, ln_g, ln_b, *, T, RC=64):
    B, L, _ = zb.shape
    hb = T // CONF_HALO
    ua, ub = U_CCA * LANES // CONV_CH, U_CCB * LANES // CONV_CH
    vec = pl.BlockSpec((1, CONV_CH), lambda b, t: (0, 0))

    def halo(u):
        return pl.BlockSpec((1, CONF_HALO, CONV_CH), lambda b, t: (b, jnp.maximum(t * hb - 1, 0), u))

    return pl.pallas_call(
        functools.partial(_conf_kernel, T=T, RC=RC),
        grid=(B, L // T),
        in_specs=[
            pl.BlockSpec((1, T, CONV_CH), lambda b, t: (b, t, ua)),
            pl.BlockSpec((1, T, CONV_CH), lambda b, t: (b, t, ub)),
            halo(ua), halo(ub),
            pl.BlockSpec((CONF_KERNEL, CONV_CH), lambda b, t: (0, 0)),
            vec, vec, vec,
        ],
        out_specs=pl.BlockSpec((1, T, CONV_CH), lambda b, t: (b, t, 0)),
        out_shape=jax.ShapeDtypeStruct((B, L, CONV_CH), BF16),
        scratch_shapes=[pltpu.VMEM((T + CONF_HALO, CONV_CH), F32)],
        compiler_params=_cparams(("parallel", "parallel")),
        name="conf",
    )(zb, zb, zb, zb, dw_w, dw_b, ln_g, ln_b)


def _outproj_kernel(fox_ref, gdn_ref, conf_ref, wf_ref, wg_ref, wc_ref, h_ref, o_ref):
    acc = jnp.dot(fox_ref[...], wf_ref[...], preferred_element_type=F32)
    acc = acc + jnp.dot(gdn_ref[...], wg_ref[...], preferred_element_type=F32)
    acc = acc + jnp.dot(conf_ref[...], wc_ref[...], preferred_element_type=F32)
    o_ref[...] = h_ref[...] + acc


def _outproj(fox_o, gdn_o, conf_o, w_out, h, *, tm, tn):
    M, D = h.shape
    W = N_HEADS * HEAD_DIM
    return pl.pallas_call(
        _outproj_kernel,
        grid=(D // tn, M // tm),
        in_specs=[
            pl.BlockSpec((tm, W), lambda j, i: (i, 0)),
            pl.BlockSpec((tm, W), lambda j, i: (i, 0)),
            pl.BlockSpec((tm, CONV_CH), lambda j, i: (i, 0)),
            pl.BlockSpec((W, tn), lambda j, i: (0, j)),
            pl.BlockSpec((W, tn), lambda j, i: (1, j)),
            pl.BlockSpec((CONV_CH, tn), lambda j, i: (2 * W // CONV_CH, j)),
            pl.BlockSpec((tm, tn), lambda j, i: (i, j)),
        ],
        out_specs=pl.BlockSpec((tm, tn), lambda j, i: (i, j)),
        out_shape=jax.ShapeDtypeStruct((M, D), F32),
        compiler_params=_cparams(("parallel", "parallel")),
        name="outproj",
    )(fox_o, gdn_o, conf_o, w_out, w_out, w_out, h)


def _ffn_kernel(h_ref, g_ref, wg_ref, wu_ref, wd_ref, o_ref, xn_ref):
    f = pl.program_id(1)

    @pl.when(f == 0)
    def _():
        x = h_ref[...]
        xn_ref[...] = _rms_rows(x, g_ref[...]).astype(BF16)
        o_ref[...] = x

    xn = xn_ref[...]
    gate = jnp.dot(xn, wg_ref[...], preferred_element_type=F32)
    up = jnp.dot(xn, wu_ref[...], preferred_element_type=F32)
    act = (gate * _sigmoid(gate) * up).astype(BF16)
    o_ref[...] += jnp.dot(act, wd_ref[...], preferred_element_type=F32)


def _ffn(h, g, wg, wu, wd, *, tm, tf):
    M, D = h.shape
    F = wg.shape[1]
    return pl.pallas_call(
        _ffn_kernel,
        grid=(M // tm, F // tf),
        in_specs=[
            pl.BlockSpec((tm, D), lambda i, f: (i, 0)),
            pl.BlockSpec((1, D), lambda i, f: (0, 0)),
            pl.BlockSpec((D, tf), lambda i, f: (0, f)),
            pl.BlockSpec((D, tf), lambda i, f: (0, f)),
            pl.BlockSpec((tf, D), lambda i, f: (f, 0)),
        ],
        out_specs=pl.BlockSpec((tm, D), lambda i, f: (i, 0)),
        out_shape=jax.ShapeDtypeStruct((M, D), F32),
        scratch_shapes=[pltpu.VMEM((tm, D), BF16)],
        compiler_params=_cparams(("parallel", "arbitrary")),
        name="ffn",
    )(h, g, wg, wu, wd)


def _router_kernel(h_ref, g_ref, rhi_ref, rlo_ref, info_ref, cnt_ref, carry_ref, *, tm):
    i = pl.program_id(0)

    @pl.when(i == 0)
    def _():
        carry_ref[...] = jnp.zeros_like(carry_ref)

    xn = _rms_rows(h_ref[...], g_ref[...])
    x_hi = xn.astype(BF16)
    x_lo = (xn - x_hi.astype(F32)).astype(BF16)
    logits = (jnp.dot(x_hi, rhi_ref[...], preferred_element_type=F32)
              + jnp.dot(x_lo, rhi_ref[...], preferred_element_type=F32)
              + jnp.dot(x_hi, rlo_ref[...], preferred_element_type=F32))
    lane = lax.broadcasted_iota(jnp.int32, (tm, LANES), 1).astype(F32)
    logits = jnp.where(lane < N_EXPERTS, logits, -BIG)
    m1 = jnp.max(logits, axis=1, keepdims=True)
    i1 = jnp.min(jnp.where(logits == m1, lane, float(LANES)), axis=1, keepdims=True)
    rest = jnp.where(lane == i1, -BIG, logits)
    m2 = jnp.max(rest, axis=1, keepdims=True)
    i2 = jnp.min(jnp.where(rest == m2, lane, float(LANES)), axis=1, keepdims=True)
    e = jnp.exp(m2 - m1)
    g1 = 1.0 / (1.0 + e)
    g2 = e / (1.0 + e)

    sel = ((lane == i1) | (lane == i2)).astype(F32)
    row = lax.broadcasted_iota(jnp.int32, (tm, tm), 0)
    col = lax.broadcasted_iota(jnp.int32, (tm, tm), 1)
    before = jnp.dot((col < row).astype(BF16), sel.astype(BF16), preferred_element_type=F32)
    ranks = before + carry_ref[...]
    r1 = jnp.sum(jnp.where(lane == i1, ranks, 0.0), axis=1, keepdims=True)
    r2 = jnp.sum(jnp.where(lane == i2, ranks, 0.0), axis=1, keepdims=True)
    carry_ref[...] += jnp.sum(sel, axis=0, keepdims=True)
    cnt_ref[...] = carry_ref[...]

    info = jnp.where(lane == 0, i1, 0.0)
    info = jnp.where(lane == 1, i2, info)
    info = jnp.where(lane == 2, g1, info)
    info = jnp.where(lane == 3, g2, info)
    info = jnp.where(lane == 4, r1, info)
    info = jnp.where(lane == 5, r2, info)
    info_ref[...] = info


def _router(h, g, r_hi, r_lo, *, tm):
    M, D = h.shape
    return pl.pallas_call(
        functools.partial(_router_kernel, tm=tm),
        grid=(M // tm,),
        in_specs=[
            pl.BlockSpec((tm, D), lambda i: (i, 0)),
            pl.BlockSpec((1, D), lambda i: (0, 0)),
            pl.BlockSpec((D, LANES), lambda i: (0, 0)),
            pl.BlockSpec((D, LANES), lambda i: (0, 0)),
        ],
        out_specs=[
            pl.BlockSpec((tm, LANES), lambda i: (i, 0)),
            pl.BlockSpec((1, LANES), lambda i: (0, 0)),
        ],
        out_shape=[jax.ShapeDtypeStruct((M, LANES), F32), jax.ShapeDtypeStruct((1, LANES), F32)],
        scratch_shapes=[pltpu.VMEM((1, LANES), F32)],
        compiler_params=_cparams(("arbitrary",)),
        name="router",
    )(h, g, r_hi, r_lo)


PACK_SHIFT = 15
PACK = 1 << PACK_SHIFT
SCATTER_GROUPS = 8


def _moe_kernel(te_ref, nv_ref, pk_ref, h_hbm, g_ref, wg_ref, wu_ref, wd_ref, y_hbm,
                xg, xn, acc, sem_g, sem_s, *, tm, nf):
    i = pl.program_id(0)
    f = pl.program_id(1)
    nv = nv_ref[0]
    valid = i < nv

    def start_gather(tile):
        def body(r, c):
            tok = pk_ref[tile * tm + r] & (PACK - 1)
            pltpu.make_async_copy(h_hbm.at[pl.ds(tok, 1), :], xg.at[pl.ds(r, 1), :], sem_g).start()
            return c
        lax.fori_loop(0, tm, body, 0, unroll=8)

    def wait_gather():
        pltpu.make_async_copy(h_hbm.at[pl.ds(0, tm), :], xg, sem_g).wait()

    def wait_scatter():
        pltpu.make_async_copy(acc, y_hbm.at[pl.ds(0, tm), :], sem_s).wait()

    @pl.when((i == 0) & (f == 0))
    def _():
        acc[...] = jnp.zeros_like(acc)
        spare = pltpu.make_async_copy(acc, y_hbm.at[pl.ds(y_hbm.shape[0] - tm, tm), :], sem_s)
        spare.start()
        spare.wait()

    @pl.when(valid & (f == 0))
    def _():
        @pl.when(i == 0)
        def _():
            start_gather(i)

        wait_gather()
        xn[...] = _rms_rows(xg[...], g_ref[...]).astype(BF16)

    n_issue = max(d for d in (1, 2, 4, 8) if d <= nf - 2 and tm % d == 0)
    chunk = tm // n_issue
    prefetch = (i + 1 < nv) & (f >= 1) & (f <= n_issue)
    last = f == nf - 1

    def issue_chunk():
        row0 = (f - 1) * chunk
        for r in range(chunk):
            tok = pk_ref[(i + 1) * tm + row0 + r] & (PACK - 1)
            pltpu.make_async_copy(h_hbm.at[pl.ds(tok, 1), :], xg.at[pl.ds(row0 + r, 1), :], sem_g).start(priority=1)

    def hidden():
        x = xn[...]
        gate = jnp.dot(x, wg_ref[0].astype(BF16), preferred_element_type=F32)
        up = jnp.dot(x, wu_ref[0].astype(BF16), preferred_element_type=F32)
        return (gate * _sigmoid(gate) * up).astype(BF16)

    @pl.when(valid & (f == 0))
    def _():
        act = hidden()

        @pl.when(i > 0)
        def _():
            wait_scatter()

        acc[...] = jnp.dot(act, wd_ref[0].astype(BF16), preferred_element_type=F32)

    @pl.when(valid & (f > 0) & prefetch)
    def _():
        issue_chunk()
        acc[...] += jnp.dot(hidden(), wd_ref[0].astype(BF16), preferred_element_type=F32)

    @pl.when(valid & (f > 0) & jnp.logical_not(prefetch) & jnp.logical_not(last))
    def _():
        acc[...] += jnp.dot(hidden(), wd_ref[0].astype(BF16), preferred_element_type=F32)

    @pl.when(valid & last)
    def _():
        act = hidden()
        wd = wd_ref[0].astype(BF16)
        group = tm // SCATTER_GROUPS
        for c in range(SCATTER_GROUPS):
            rows = slice(c * group, (c + 1) * group)
            acc[rows, :] += jnp.dot(act[rows], wd, preferred_element_type=F32)
            for r in range(c * group, (c + 1) * group):
                dst = pk_ref[i * tm + r] >> PACK_SHIFT
                pltpu.make_async_copy(acc.at[pl.ds(r, 1), :], y_hbm.at[pl.ds(dst, 1), :], sem_s).start(priority=r % 2)

        @pl.when(i == nv - 1)
        def _():
            wait_scatter()


def _moe(tile_expert, n_valid, packed, h, g, wg, wu, wd, *, tm, tf, n_tiles):
    M, D = h.shape
    F = wg.shape[2]
    nf = F // tf
    assert nf >= 3 and tm % SCATTER_GROUPS == 0

    def fblk(i, f, nv):
        return jnp.where(i < nv[0], f, nf - 1)

    return pl.pallas_call(
        functools.partial(_moe_kernel, tm=tm, nf=nf),
        grid_spec=pltpu.PrefetchScalarGridSpec(
            num_scalar_prefetch=3,
            grid=(n_tiles, nf),
            in_specs=[
                pl.BlockSpec(memory_space=pl.ANY),
                pl.BlockSpec((1, D), lambda i, f, te, nv, pk: (0, 0)),
                pl.BlockSpec((1, D, tf), lambda i, f, te, nv, pk: (te[i], 0, fblk(i, f, nv))),
                pl.BlockSpec((1, D, tf), lambda i, f, te, nv, pk: (te[i], 0, fblk(i, f, nv))),
                pl.BlockSpec((1, tf, D), lambda i, f, te, nv, pk: (te[i], fblk(i, f, nv), 0)),
            ],
            out_specs=pl.BlockSpec(memory_space=pl.ANY),
            scratch_shapes=[
                pltpu.VMEM((tm, D), F32), pltpu.VMEM((tm, D), BF16), pltpu.VMEM((tm, D), F32),
                pltpu.SemaphoreType.DMA, pltpu.SemaphoreType.DMA,
            ],
        ),
        out_shape=jax.ShapeDtypeStruct((2 * M + tm, D), F32),
        compiler_params=_cparams(("arbitrary", "arbitrary")),
        name="moe",
    )(tile_expert, n_valid, packed, h, g, wg, wu, wd)


def _combine_kernel(h_ref, y0_ref, y1_ref, info_ref, gf_ref, o_ref):
    info = info_ref[...]
    hn = h_ref[...] + info[:, 2:3] * y0_ref[...] + info[:, 3:4] * y1_ref[...]
    o_ref[...] = _rms_rows(hn, gf_ref[...])


def _combine(h, y, info, gf, *, batch, seq_len):
    M, D = h.shape
    tb = FRONT
    per_seq = seq_len // tb
    real = per_seq - 1

    def src(off):
        return lambda b, j: (off + b * per_seq + 1 + j, 0)

    return pl.pallas_call(
        _combine_kernel,
        grid=(batch, real),
        in_specs=[
            pl.BlockSpec((tb, D), src(0)),
            pl.BlockSpec((tb, D), src(0)),
            pl.BlockSpec((tb, D), src(M // tb)),
            pl.BlockSpec((tb, LANES), src(0)),
            pl.BlockSpec((1, D), lambda b, j: (0, 0)),
        ],
        out_specs=pl.BlockSpec((tb, D), lambda b, j: (b * real + j, 0)),
        out_shape=jax.ShapeDtypeStruct((batch * real * tb, D), F32),
        compiler_params=_cparams(("parallel", "parallel")),
        name="combine",
    )(h, y, y, info, gf)


def _prep_w_in(w):
    W = N_HEADS * HEAD_DIM
    sizes = [W, W, W, N_HEADS, W, W, W, N_HEADS, N_HEADS, W, 2 * CONV_CH]
    parts, o = [], 0
    for s in sizes:
        parts.append(w[:, o:o + s])
        o += s
    fq, fk, fv, ff, gq, gk, gv, ga, gb, gg, cc = parts
    main = jnp.concatenate([cc, fq * (HEAD_DIM ** -0.5 * LOG2E), fk, fv, gq, gk, gv, gg], axis=1).astype(BF16)
    small = jnp.concatenate([ff, ga, gb], axis=1)
    small = jnp.pad(small, ((0, 0), (0, LANES - small.shape[1]))).astype(BF16)
    return main, small


def _lane_vec(pairs):
    v = jnp.zeros((1, LANES), F32)
    for lane0, vals in pairs:
        v = lax.dynamic_update_slice(v, vals.astype(F32)[None, :], (0, lane0))
    return v


def _mixer(h, layer, p, *, B, L, T, tm):
    zb, zs = _inproj(h, p["attn_norm_g"][layer][None, :], *p["w_in"][layer], seq_len=L, tm=tm, tn=1280)
    zb = zb.reshape(B, L, N_MAIN)
    bias_vec = _lane_vec([(LANE_FF, p["fox_b_f"][layer]), (LANE_GA, p["gdn_dt_bias"][layer])])
    alog_vec = _lane_vec([(LANE_GA, p["gdn_A_log"][layer])])
    gates = _gates(zs.reshape(B, L, LANES), bias_vec, alog_vec, T=T)
    gates_t = jnp.swapaxes(gates[:, :, :24], 1, 2)
    grow = gates_t.reshape(B, 24, L // GDN_CHUNK, GDN_CHUNK)
    fox_o = _fox(zb, gates, T=T)
    conv_w = p["gdn_conv_w"][layer].reshape(GDN_CONV, 3 * N_HEADS, HEAD_DIM).transpose(1, 0, 2)
    gdn_o = _gdn(zb, gates, grow, conv_w, p["gdn_norm_g"][layer][None, :], T=T)
    conf_o = _conf(zb, p["cc_dw_w"][layer], p["cc_dw_b"][layer][None, :], p["cc_ln_g"][layer][None, :],
                   p["cc_ln_b"][layer][None, :], T=T)
    M = B * L
    return _outproj(fox_o.reshape(M, -1), gdn_o.reshape(M, -1), conf_o.reshape(M, -1),
                    p["w_out"][layer], h, tm=tm, tn=1024)


def _moe_layer(h, i, p, *, B, L, tm_r, tm_e, tf_e):
    M, D = h.shape
    g = p["ffn_norm_g_moe"][i][None, :]
    router = jnp.pad(p["moe_router"][i], ((0, 0), (0, LANES - N_EXPERTS)))
    r_hi = router.astype(BF16)
    r_lo = (router - r_hi.astype(F32)).astype(BF16)
    info, cnt = _router(h, g, r_hi, r_lo, tm=tm_r)

    e = info[:, 0:2].astype(jnp.int32)
    rank = info[:, 4:6].astype(jnp.int32)
    counts = cnt[0, :N_EXPERTS].astype(jnp.int32)
    padded = ((counts + tm_e - 1) // tm_e) * tm_e
    ends = jnp.cumsum(padded)
    offs = ends - padded
    pos = offs[e] + rank
    n_tiles = (2 * M + N_EXPERTS * (tm_e - 1)) // tm_e
    P = n_tiles * tm_e
    tok = jnp.arange(M, dtype=jnp.int32)[:, None]
    dst = jnp.arange(2, dtype=jnp.int32)[None, :] * M + tok
    word = dst * PACK + tok
    filler = (2 * M + jnp.arange(P, dtype=jnp.int32) % tm_e) * PACK
    packed = filler.at[pos.reshape(-1)].set(word.reshape(-1))
    n_valid = (ends[-1] // tm_e).astype(jnp.int32)
    tile_start = jnp.minimum(jnp.arange(n_tiles, dtype=jnp.int32), n_valid - 1) * tm_e
    tile_expert = jnp.minimum(jnp.searchsorted(ends, tile_start, side="right"),
                              N_EXPERTS - 1).astype(jnp.int32)

    y = _moe(tile_expert, n_valid[None], packed, h, g, p["moe_w_gate"][i], p["moe_w_up"][i],
             p["moe_w_down"][i], tm=tm_e, tf=tf_e, n_tiles=n_tiles)
    return _combine(h, y, info, p["final_norm_g"][None, :], batch=B, seq_len=L)


def kernel(x, meta_tokens, attn_norm_g, w_in, fox_b_f, gdn_conv_w, gdn_A_log, gdn_dt_bias, gdn_norm_g,
           cc_dw_w, cc_dw_b, cc_ln_g, cc_ln_b, w_out, ffn_norm_g, dense_w_gate, dense_w_up,
           dense_w_down, moe_router, moe_w_gate, moe_w_up, moe_w_down, final_norm_g):
    B, S, D = x.shape
    depth = w_in.shape[0]
    assert depth == 2, "layer 0 is dense SwiGLU, layer 1 is the routed expert SwiGLU followed by the final norm"
    L = S + FRONT
    T = 640 if L % 640 == 0 else 128
    tm = T
    p = dict(
        attn_norm_g=attn_norm_g, fox_b_f=fox_b_f, gdn_conv_w=gdn_conv_w, gdn_A_log=gdn_A_log,
        gdn_dt_bias=gdn_dt_bias, gdn_norm_g=gdn_norm_g, cc_dw_w=cc_dw_w, cc_dw_b=cc_dw_b,
        cc_ln_g=cc_ln_g, cc_ln_b=cc_ln_b, w_out=w_out.astype(BF16),
        w_in=[_prep_w_in(w_in[l]) for l in range(depth)],
        ffn_norm_g_moe=ffn_norm_g[1::2], moe_router=moe_router,
        moe_w_gate=moe_w_gate, moe_w_up=moe_w_up, moe_w_down=moe_w_down, final_norm_g=final_norm_g,
    )
    pad = jnp.zeros((B, N_PAD, D), x.dtype)
    meta = jnp.broadcast_to(meta_tokens.astype(x.dtype)[None], (B, N_META, D))
    h = jnp.concatenate([pad, meta, x], axis=1).reshape(B * L, D)

    h = _mixer(h, 0, p, B=B, L=L, T=T, tm=tm)
    h = _ffn(h, ffn_norm_g[0][None, :], dense_w_gate[0].astype(BF16), dense_w_up[0].astype(BF16),
             dense_w_down[0].astype(BF16), tm=tm, tf=512)
    h = _mixer(h, 1, p, B=B, L=L, T=T, tm=tm)
    out = _moe_layer(h, 0, p, B=B, L=L, tm_r=tm, tm_e=1024, tf_e=512)
    return out.reshape(B, S, D)
```
